```python
import jax, jax.numpy as jnp
from jax import lax
import numpy as np

D_MODEL = 1024
BATCH = 4
SEQ = 4096
DEPTH = 4

CHUNK = 64
Q_BLOCK = 128
D_FF = 2816
EPS = 1e-6

MLA_HEADS = 8
MLA_NOPE = 64
MLA_ROPE = 32
MLA_QK = MLA_NOPE + MLA_ROPE
MLA_V = 64
Q_LORA = 256
KV_LORA = 128
ROPE_BASE = 10000.0

RWKV_HEADS = 8
RWKV_HEAD = 64
RWKV_W = RWKV_HEADS * RWKV_HEAD
DECAY_LORA = 64
A_LORA = 64
GATE_LORA = 160
GN_EPS = 64e-5

MLA_W = MLA_HEADS * MLA_V
MIX_W = MLA_W + RWKV_W
MLA_IN = Q_LORA + KV_LORA + MLA_ROPE
RWKV_SIZES = (RWKV_W, DECAY_LORA, RWKV_W, RWKV_W, A_LORA, GATE_LORA)
RWKV_IN = sum(RWKV_SIZES)
N_IN = MLA_IN + RWKV_IN

kernel_name = 'hybrid_mla_rwkv7_macaron_encoder'

f32 = jnp.float32


def _split(x, sizes):
    idx, acc = [], 0
    for s in sizes[:-1]:
        acc += s
        idx.append(acc)
    return jnp.split(x, idx, axis=-1)


def _rms_norm(x, g):
    xf = x.astype(f32)
    y = xf * lax.rsqrt(jnp.mean(xf * xf, axis=-1, keepdims=True) + EPS)
    return (y * g.astype(f32)).astype(x.dtype)


def _swiglu(h, w_gate, w_up, w_down):
    return (jax.nn.silu(h @ w_gate) * (h @ w_up)) @ w_down


def _rope_tables(seq):
    pos = jnp.arange(seq, dtype=f32)
    inv_freq = ROPE_BASE ** (-jnp.arange(0, MLA_ROPE, 2, dtype=f32) / MLA_ROPE)
    ang = pos[:, None] * inv_freq[None, :]
    return jnp.cos(ang)[None, :, None, :], jnp.sin(ang)[None, :, None, :]


def _rope_tail(x, cos, sin):
    nope, pe = x[..., :MLA_NOPE], x[..., MLA_NOPE:].astype(f32)
    half = MLA_ROPE // 2
    x1, x2 = pe[..., :half], pe[..., half:]
    rot = jnp.concatenate([x1 * cos - x2 * sin, x2 * cos + x1 * sin], axis=-1).astype(x.dtype)
    return jnp.concatenate([nope, rot], axis=-1)


def _chunk_causal_attention(q, k, v):
    B, S, H, Dq = q.shape
    Dv = v.shape[-1]
    nb = S // Q_BLOCK
    scale = MLA_QK ** -0.5
    qt = q.transpose(0, 2, 1, 3).reshape(B, H, nb, Q_BLOCK, Dq).transpose(2, 0, 1, 3, 4)
    kt = k.transpose(0, 2, 1, 3)
    vt = v.transpose(0, 2, 1, 3)
    key_chunk = jnp.arange(S) // CHUNK

    def block(args):
        qb, i = args
        s = jnp.einsum('bhqd,bhkd->bhqk', qb, kt).astype(f32) * scale
        q_chunk = (i * Q_BLOCK + jnp.arange(Q_BLOCK)) // CHUNK
        mask = key_chunk[None, :] <= q_chunk[:, None]
        s = jnp.where(mask[None, None], s, -jnp.inf)
        pr = jax.nn.softmax(s, axis=-1).astype(vt.dtype)
        return jnp.einsum('bhqk,bhkd->bhqd', pr, vt)

    o = lax.map(block, (qt, jnp.arange(nb)))
    return o.transpose(1, 0, 3, 2, 4).reshape(B, S, H * Dv)


def _mla(p, q_lat_norm, w_q_up, kv_lat_norm, w_kv_up, q_norm, k_norm, cos, sin):
    B, S, _ = p.shape
    q_lat, kv_lat, k_pe = _split(p, [Q_LORA, KV_LORA, MLA_ROPE])
    q = (_rms_norm(q_lat, q_lat_norm) @ w_q_up).reshape(B, S, MLA_HEADS, MLA_QK)
    kv = (_rms_norm(kv_lat, kv_lat_norm) @ w_kv_up).reshape(B, S, MLA_HEADS, MLA_NOPE + MLA_V)
    k_nope, v = kv[..., :MLA_NOPE], kv[..., MLA_NOPE:]
    k = jnp.concatenate([k_nope, jnp.broadcast_to(k_pe[:, :, None, :], (B, S, MLA_HEADS, MLA_ROPE))], axis=-1)
    q = _rope_tail(_rms_norm(q, q_norm), cos, sin)
    k = _rope_tail(_rms_norm(k, k_norm), cos, sin)
    return _chunk_causal_attention(q, k, v)


def _rwkv7(p, mu, w0, w2, a0, a2, g2, k_k, k_a, r_k, gn_gain, gn_bias):
    B, S, _ = p.shape
    prev = jnp.pad(p[:, :-1], ((0, 0), (1, 0), (0, 0)))
    p = p + mu * (prev - p)
    r, w_l, k, v, a_l, g_l = _split(p, list(RWKV_SIZES))
    w = -jax.nn.softplus(-(w0 + jnp.tanh(w_l) @ w2).astype(f32)) - 0.5
    decay = jnp.exp(-jnp.exp(w))
    a = jax.nn.sigmoid((a0 + a_l @ a2).astype(f32))
    g = jax.nn.sigmoid(g_l) @ g2

    def heads(t):
        return t.astype(f32).reshape(B, S, RWKV_HEADS, RWKV_HEAD)

    r, k, v, a, decay = heads(r), heads(k), heads(v), heads(a), heads(decay)
    kk = k * k_k.astype(f32).reshape(RWKV_HEADS, RWKV_HEAD)
    kk = kk / jnp.maximum(jnp.sqrt(jnp.sum(kk * kk, axis=-1, keepdims=True)), 1e-12)
    k = k * (1.0 + (a - 1.0) * k_a.astype(f32).reshape(RWKV_HEADS, RWKV_HEAD))
    xs = tuple(jnp.moveaxis(t, 1, 0) for t in (r, decay, k, v, -kk, kk * a))

    def step(state, inp):
        r_t, w_t, k_t, v_t, a_t, b_t = inp
        sa = jnp.einsum('bhvk,bhk->bhv', state, a_t)
        state = state * w_t[:, :, None, :] + sa[..., None] * b_t[:, :, None, :] + v_t[..., None] * k_t[:, :, None, :]
        return state, jnp.einsum('bhvk,bhk->bhv', state, r_t)

    state0 = jnp.zeros((B, RWKV_HEADS, RWKV_HEAD, RWKV_HEAD), f32)
    _, y = lax.scan(step, state0, xs)
    y = jnp.moveaxis(y, 0, 1)
    mean = jnp.mean(y, axis=-1, keepdims=True)
    var = jnp.mean((y - mean) ** 2, axis=-1, keepdims=True)
    y = ((y - mean) * lax.rsqrt(var + GN_EPS)).reshape(B, S, RWKV_W)
    y = y * gn_gain.astype(f32) + gn_bias.astype(f32)
    bonus = jnp.sum(r * k * r_k.astype(f32), axis=-1, keepdims=True) * v
    y = (y + bonus.reshape(B, S, RWKV_W)) * g.astype(f32)
    return y.astype(p.dtype)


def setup_inputs(seed: int = 0) -> dict:
    key = jax.random.key(seed)
    ks = jax.random.split(key, 40)
    cnt = [0]
    L = DEPTH

    def nxt():
        kk = ks[cnt[0]]
        cnt[0] += 1
        return kk

    def nrm(shape, scale):
        return jax.random.normal(nxt(), shape, jnp.float32) * scale

    def gain(shape):
        return 1.0 + nrm(shape, 0.05)

    def unif(shape, lo, hi):
        return jax.random.uniform(nxt(), shape, jnp.float32, lo, hi)

    return {
        'x': nrm((BATCH, SEQ, D_MODEL), 1.0),
        'ffn1_norm': gain((L, D_MODEL)),
        'ffn1_w_gate': nrm((L, D_MODEL, D_FF), D_MODEL ** -0.5),
        'ffn1_w_up': nrm((L, D_MODEL, D_FF), D_MODEL ** -0.5),
        'ffn1_w_down': nrm((L, D_FF, D_MODEL), D_FF ** -0.5),
        'mix_norm': gain((L, D_MODEL)),
        'w_in': nrm((L, D_MODEL, N_IN), D_MODEL ** -0.5),
        'q_lat_norm': gain((L, Q_LORA)),
        'w_q_up': nrm((L, Q_LORA, MLA_HEADS * MLA_QK), Q_LORA ** -0.5),
        'kv_lat_norm': gain((L, KV_LORA)),
        'w_kv_up': nrm((L, KV_LORA, MLA_HEADS * (MLA_NOPE + MLA_V)), KV_LORA ** -0.5),
        'q_norm': gain((L, MLA_QK)),
        'k_norm': gain((L, MLA_QK)),
        'shift_mu': unif((L, RWKV_IN), 0.0, 1.0),
        'w0': unif((L, RWKV_W), -4.0, 0.0),
        'w2': nrm((L, DECAY_LORA, RWKV_W), 0.5 * DECAY_LORA ** -0.5),
        'a0': nrm((L, RWKV_W), 0.1),
        'a2': nrm((L, A_LORA, RWKV_W), A_LORA ** -0.5),
        'g2': nrm((L, GATE_LORA, RWKV_W), GATE_LORA ** -0.5),
        'k_k': 0.85 + nrm((L, RWKV_W), 0.05),
        'k_a': gain((L, RWKV_W)),
        'r_k': nrm((L, RWKV_HEADS, RWKV_HEAD), 0.1),
        'gn_gain': gain((L, RWKV_W)),
        'gn_bias': nrm((L, RWKV_W), 0.02),
        'out_norm_mla': gain((L, MLA_W)),
        'out_norm_rwkv': gain((L, RWKV_W)),
        'w_out': nrm((L, MIX_W, D_MODEL), MIX_W ** -0.5),
        'ffn2_norm': gain((L, D_MODEL)),
        'ffn2_w_gate': nrm((L, D_MODEL, D_FF), D_MODEL ** -0.5),
        'ffn2_w_up': nrm((L, D_MODEL, D_FF), D_MODEL ** -0.5),
        'ffn2_w_down': nrm((L, D_FF, D_MODEL), D_FF ** -0.5),
    }


def reference(x, ffn1_norm, ffn1_w_gate, ffn1_w_up, ffn1_w_down, mix_norm, w_in,
              q_lat_norm, w_q_up, kv_lat_norm, w_kv_up, q_norm, k_norm,
              shift_mu, w0, w2, a0, a2, g2, k_k, k_a, r_k, gn_gain, gn_bias,
              out_norm_mla, out_norm_rwkv, w_out,
              ffn2_norm, ffn2_w_gate, ffn2_w_up, ffn2_w_down):
    S = x.shape[1]
    cos, sin = _rope_tables(S)
    for l in range(DEPTH):
        x = x + 0.5 * _swiglu(_rms_norm(x, ffn1_norm[l]), ffn1_w_gate[l], ffn1_w_up[l], ffn1_w_down[l])
        p = _rms_norm(x, mix_norm[l]) @ w_in[l]
        y_mla = _mla(p[..., :MLA_IN], q_lat_norm[l], w_q_up[l], kv_lat_norm[l], w_kv_up[l],
                     q_norm[l], k_norm[l], cos, sin)
        y_rwkv = _rwkv7(p[..., MLA_IN:], shift_mu[l], w0[l], w2[l], a0[l], a2[l], g2[l],
                        k_k[l], k_a[l], r_k[l], gn_gain[l], gn_bias[l])
        y = jnp.concatenate([_rms_norm(y_mla, out_norm_mla[l]), _rms_norm(y_rwkv, out_norm_rwkv[l])], axis=-1)
        x = x + y @ w_out[l]
        x = x + 0.5 * _swiglu(_rms_norm(x, ffn2_norm[l]), ffn2_w_gate[l], ffn2_w_up[l], ffn2_w_down[l])
    return x
```

```python
import functools

import jax
import jax.numpy as jnp
from jax import lax
from jax.experimental import pallas as pl
from jax.experimental.pallas import tpu as pltpu

f32 = jnp.float32
bf16 = jnp.bfloat16

D_MODEL = 1024
DEPTH = 4
CHUNK = 64
D_FF = 2816
EPS = 1e-6
MLA_HEADS = 8
MLA_NOPE = 64
MLA_ROPE = 32
MLA_QK = MLA_NOPE + MLA_ROPE
MLA_V = 64
Q_LORA = 256
KV_LORA = 128
ROPE_BASE = 10000.0
RWKV_HEADS = 8
RWKV_HEAD = 64
RWKV_W = RWKV_HEADS * RWKV_HEAD
DECAY_LORA = 64
A_LORA = 64
GATE_LORA = 160
GN_EPS = 64e-5
MLA_W = MLA_HEADS * MLA_V
MLA_IN = Q_LORA + KV_LORA + MLA_ROPE

LANES = 128
HEAD_PAD = LANES
QK_PAD_W = MLA_HEADS * HEAD_PAD
P_MLA_W = Q_LORA + KV_LORA + 2 * LANES
GATE_PAD = 2 * LANES
P_RWKV_W = 3 * RWKV_W + LANES + GATE_PAD
VMEM_LIMIT = 56 * 1024 * 1024

TM_FFN = 256
TM_PREP = 512
TQ = 256
TM_RWKV = 256
N_PAIR = RWKV_HEADS // 2


def _rms(x, g):
    return x * lax.rsqrt(jnp.mean(x * x, axis=-1, keepdims=True) + EPS) * g


def _dot(a, b):
    return jnp.dot(a, b, preferred_element_type=f32)


def _dot_nt(a, b):
    return lax.dot_general(a, b, (((1,), (1,)), ((), ())), preferred_element_type=f32)


def _dot_tn(a, b):
    return lax.dot_general(a, b, (((0,), (0,)), ((), ())), preferred_element_type=f32)


def _dot_split(c, x):
    hi = x.astype(bf16)
    lo = (x - hi.astype(f32)).astype(bf16)
    return _dot(c, hi) + _dot(c, lo)


def _dot_split_r(x, c):
    hi = x.astype(bf16)
    lo = (x - hi.astype(f32)).astype(bf16)
    return _dot(hi, c) + _dot(lo, c)


def _swiglu_half(x, g, wg_ref, wu_ref, wd_ref):
    h = _rms(x, g).astype(bf16)
    gate = _dot(h, wg_ref[...])
    up = _dot(h, wu_ref[...])
    act = (jax.nn.silu(gate) * up).astype(bf16)
    return x + 0.5 * _dot(act, wd_ref[...])


def _ffn_in_kernel(x_ref, g1_ref, wg_ref, wu_ref, wd_ref, gm_ref, win_ref, x1_ref, pm_ref, pr_ref):
    x1 = _swiglu_half(x_ref[...], g1_ref[...], wg_ref, wu_ref, wd_ref)
    x1_ref[...] = x1
    p = _dot(_rms(x1, gm_ref[...]).astype(bf16), win_ref[...])
    pm_ref[...] = p[:, :P_MLA_W]
    pr_ref[...] = p[:, P_MLA_W:]


def _out_ffn_kernel(x_ref, o_ref, y_ref, go_ref, gy_ref, wo_ref, g2_ref, wg_ref, wu_ref, wd_ref, x3_ref):
    ho = _rms(o_ref[...], go_ref[...]).astype(bf16)
    hy = _rms(y_ref[...], gy_ref[...]).astype(bf16)
    x2 = x_ref[...] + _dot(ho, wo_ref[:MLA_W, :]) + _dot(hy, wo_ref[MLA_W:, :])
    x3_ref[...] = _swiglu_half(x2, g2_ref[...], wg_ref, wu_ref, wd_ref)


def _mla_prep_kernel(pm_ref, cos_ref, sin_ref, gq_ref, gkv_ref, wq_ref, wkv_ref, qn_ref, kn_ref,
                     q_ref, k_ref, v_ref):
    pm = pm_ref[...]
    q_lat = pm[:, :Q_LORA]
    kv_lat = pm[:, Q_LORA:Q_LORA + KV_LORA]
    kpe = pm[:, Q_LORA + KV_LORA:Q_LORA + KV_LORA + LANES]
    kpe_sw = pm[:, Q_LORA + KV_LORA + LANES:]
    cos_t = cos_ref[...]
    sin_t = sin_ref[...]
    q2 = _dot(_rms(q_lat, gq_ref[...]).astype(bf16), wq_ref[...])
    kv = _dot(_rms(kv_lat, gkv_ref[...]).astype(bf16), wkv_ref[...])
    qc = qn_ref[0:1, :] * cos_t
    qs = qn_ref[1:2, :] * sin_t
    kc = kn_ref[0:1, :] * cos_t
    ks = kn_ref[1:2, :] * sin_t
    scale = MLA_QK ** -0.5
    for h in range(MLA_HEADS):
        sl = slice(h * HEAD_PAD, (h + 1) * HEAD_PAD)
        qh = q2[:, sl]
        qsw = q2[:, QK_PAD_W + h * HEAD_PAD:QK_PAD_W + (h + 1) * HEAD_PAD]
        rs = lax.rsqrt(jnp.sum(qh * qh, axis=-1, keepdims=True) * (1.0 / MLA_QK) + EPS)
        q_ref[:, sl] = ((qh * qc + qsw * qs) * (rs * scale)).astype(bf16)
        kh = kv[:, sl] + kpe
        rk = lax.rsqrt(jnp.sum(kh * kh, axis=-1, keepdims=True) * (1.0 / MLA_QK) + EPS)
        k_ref[:, sl] = ((kh * kc + kpe_sw * ks) * rk).astype(bf16)
    v_ref[...] = kv[:, QK_PAD_W:].astype(bf16)


def _attn_kernel(q_ref, k_ref, v_ref, o_ref):
    i = pl.program_id(2)
    row_chunk = lax.broadcasted_iota(jnp.int32, (TQ, TQ), 0) // CHUNK
    col_chunk = lax.broadcasted_iota(jnp.int32, (TQ, TQ), 1) // CHUNK
    diag_mask = col_chunk <= row_chunk
    lane = lax.broadcasted_iota(jnp.int32, (TQ, LANES), 1)
    outs = []
    for h in range(2):
        q = q_ref[:, h * HEAD_PAD:(h + 1) * HEAD_PAD]

        def block(j, carry, masked):
            m, l, acc = carry
            rows = pl.ds(pl.multiple_of(j * TQ, TQ), TQ)
            k = k_ref[rows, h * HEAD_PAD:(h + 1) * HEAD_PAD]
            s = _dot_nt(q, k)
            if masked:
                s = jnp.where(diag_mask, s, -jnp.inf)
            m_new = jnp.maximum(m, jnp.max(s, axis=-1, keepdims=True))
            alpha = jnp.exp(m - m_new)
            pr = jnp.exp(s - m_new)
            l = alpha * l + jnp.sum(pr, axis=-1, keepdims=True)
            acc = alpha * acc + _dot(pr.astype(bf16), v_ref[rows, :])
            return m_new, l, acc

        init = (jnp.full((TQ, 1), -jnp.inf, f32), jnp.zeros((TQ, 1), f32), jnp.zeros((TQ, LANES), f32))
        carry = lax.fori_loop(0, i, functools.partial(block, masked=False), init)
        _, l, acc = block(i, carry, True)
        outs.append(acc / l)
    o_ref[...] = jnp.where(lane < MLA_V, outs[0], outs[1])


def _rwkv_kernel(p_ref, mu_ref, w0_ref, a0_ref, wwa_ref, g2_ref, kk_ref, ka_ref, rk_ref, gng_ref, gnb_ref,
                 e64_ref, tri_ref, ones_ref, y_ref,
                 shift_scr, state_scr, at_scr, rt_scr, bt_scr, kt_scr, bp_scr, kp_scr, v_scr, pc_scr, y_scr):
    tm = TM_RWKV
    i = pl.program_id(1)

    @pl.when(i == 0)
    def _():
        shift_scr[0:8, :] = jnp.zeros((8, P_RWKV_W), f32)
        state_scr[...] = jnp.zeros_like(state_scr)

    cur = p_ref[...]
    shift_scr[8:8 + tm, :] = cur
    prev = shift_scr[7:7 + tm, :]
    xs = cur + mu_ref[...] * (prev - cur)
    shift_scr[0:8, :] = shift_scr[tm:tm + 8, :]

    r = xs[:, 0:RWKV_W]
    k = xs[:, RWKV_W:2 * RWKV_W]
    v = xs[:, 2 * RWKV_W:3 * RWKV_W]
    wa = xs[:, 3 * RWKV_W:3 * RWKV_W + LANES]
    gl = xs[:, 3 * RWKV_W + LANES:]

    lane = lax.broadcasted_iota(jnp.int32, (tm, LANES), 1)
    wa = jnp.where(lane < DECAY_LORA, jnp.tanh(wa), wa)
    wa_pre = _dot(wa.astype(bf16), wwa_ref[...])
    z = -(w0_ref[...] + wa_pre[:, :RWKV_W])
    softplus = jnp.maximum(z, 0.0) + jnp.log(1.0 + jnp.exp(-jnp.abs(z)))
    logd = -jnp.exp(-softplus - 0.5)
    a = jax.nn.sigmoid(a0_ref[...] + wa_pre[:, RWKV_W:])
    g = _dot(jax.nn.sigmoid(gl).astype(bf16), g2_ref[...])

    e64 = e64_ref[...]
    kk = k * kk_ref[...]
    kk = kk / jnp.maximum(jnp.sqrt(_dot_split_r(kk * kk, e64)), 1e-12)
    kmod = k * (1.0 + (a - 1.0) * ka_ref[...])
    bonus = _dot_split_r(r * kmod * rk_ref[...], e64) * v

    cl = _dot_split(tri_ref[...], logd)
    cl_end = _dot_split(ones_ref[...], logd)
    at_scr[...] = (-kk * jnp.exp(cl - logd)).astype(bf16)
    rt_scr[...] = (r * jnp.exp(cl)).astype(bf16)
    pinv = jnp.exp(-cl)
    bt_scr[...] = (kk * a * pinv).astype(bf16)
    kt_scr[...] = (kmod * pinv).astype(bf16)
    q_end = jnp.exp(cl_end - cl)
    bp_scr[...] = (kk * a * q_end).astype(bf16)
    kp_scr[...] = (kmod * q_end).astype(bf16)
    v_scr[...] = v.astype(bf16)
    pc_scr[...] = jnp.exp(cl_end)

    c2 = 2 * CHUNK
    lane_c = lax.broadcasted_iota(jnp.int32, (CHUNK, c2), 1)
    row_c = lax.broadcasted_iota(jnp.int32, (CHUNK, c2), 0)
    left = lane_c < CHUNK
    s_idx = jnp.where(left, lane_c, lane_c - CHUNK)
    strict = s_idx < row_c
    incl = s_idx <= row_c
    eye_sbs = (s_idx == row_c).astype(f32)
    level_masks = [((row_c >> (lvl + 1)) == (s_idx >> (lvl + 1))) & (((row_c >> lvl) & 1) == 1)
                   & (((s_idx >> lvl) & 1) == 0) for lvl in range(6)]
    bd_mask = ((lax.broadcasted_iota(jnp.int32, (c2, c2), 0) // CHUNK)
               == (lax.broadcasted_iota(jnp.int32, (c2, c2), 1) // CHUNK))

    def stack2(x):
        zero = jnp.zeros_like(x)
        return jnp.concatenate([jnp.where(left, x, zero), jnp.where(left, zero, x)], axis=0)

    def chunk_step(c, carry):
        rows = pl.ds(pl.multiple_of(c * CHUNK, CHUNK), CHUNK)
        for p in range(N_PAIR):
            cols = slice(p * LANES, (p + 1) * LANES)
            at = at_scr[rows, cols]
            rt = rt_scr[rows, cols]
            vb = v_scr[rows, cols]
            ht = state_scr[p]
            ar = jnp.concatenate([at, rt], axis=0)
            yk = jnp.concatenate([stack2(bt_scr[rows, cols]), stack2(kt_scr[rows, cols])], axis=0)
            gm = _dot_nt(ar, yk)
            a_ab = gm[:CHUNK, :c2]
            a_ak = jnp.where(strict, gm[:CHUNK, c2:], 0.0)
            a_rb = jnp.where(incl, gm[CHUNK:, :c2], 0.0)
            a_rk = jnp.where(incl, gm[CHUNK:, c2:], 0.0)
            t_inv = eye_sbs + jnp.where(level_masks[0], a_ab, 0.0)
            for lvl in range(1, 6):
                n_k = jnp.where(level_masks[lvl], a_ab, 0.0).astype(bf16)
                nt = _dot(n_k, stack2(t_inv.astype(bf16)))
                t_inv = t_inv + _dot(t_inv.astype(bf16), stack2(nt.astype(bf16)))
            arh = _dot_nt(ar, ht.astype(bf16))
            vst = stack2(vb)
            w = arh[:CHUNK] + _dot(a_ak.astype(bf16), vst)
            u = _dot(t_inv.astype(bf16), stack2(w.astype(bf16)))
            ub = u.astype(bf16)
            y = arh[CHUNK:] + _dot(jnp.concatenate([a_rb, a_rk], axis=1).astype(bf16),
                                   jnp.concatenate([stack2(ub), vst], axis=0))
            y_scr[rows, cols] = y
            upd = _dot_tn(jnp.concatenate([ub, vb], axis=0),
                          jnp.concatenate([bp_scr[rows, cols], kp_scr[rows, cols]], axis=0))
            state_scr[p] = ht * pc_scr[pl.ds(pl.multiple_of(c * CHUNK, CHUNK), 1), cols] + jnp.where(bd_mask, upd, 0.0)
        return carry

    lax.fori_loop(0, tm // CHUNK, chunk_step, 0)

    y = y_scr[...]
    inv_n = 1.0 / RWKV_HEAD
    mean = _dot_split_r(y, e64) * inv_n
    yc = y - mean
    var = _dot_split_r(yc * yc, e64) * inv_n
    yn = yc * lax.rsqrt(var + GN_EPS) * gng_ref[...] + gnb_ref[...]
    y_ref[...] = (yn + bonus) * g


def _const_spec(shape):
    return pl.BlockSpec(shape, lambda *_: (0,) * len(shape), pipeline_mode=pl.Buffered(1))


def _params(sem):
    return pltpu.CompilerParams(dimension_semantics=sem, vmem_limit_bytes=VMEM_LIMIT)


def _ffn_in(x, g1, wg, wu, wd, gm, win):
    t = x.shape[0]
    tm = TM_FFN
    row = lambda w: pl.BlockSpec((tm, w), lambda i: (i, 0))
    return pl.pallas_call(
        _ffn_in_kernel,
        grid=(t // tm,),
        in_specs=[row(D_MODEL), _const_spec(g1.shape), _const_spec(wg.shape), _const_spec(wu.shape),
                  _const_spec(wd.shape), _const_spec(gm.shape), _const_spec(win.shape)],
        out_specs=[row(D_MODEL), row(P_MLA_W), row(P_RWKV_W)],
        out_shape=[jax.ShapeDtypeStruct((t, D_MODEL), f32), jax.ShapeDtypeStruct((t, P_MLA_W), f32),
                   jax.ShapeDtypeStruct((t, P_RWKV_W), f32)],
        compiler_params=_params(("parallel",)),
        name="ffn_in",
    )(x, g1, wg, wu, wd, gm, win)


def _out_ffn(x1, o, y, go, gy, wo, g2, wg, wu, wd):
    t = x1.shape[0]
    tm = TM_FFN
    row = lambda w: pl.BlockSpec((tm, w), lambda i: (i, 0))
    return pl.pallas_call(
        _out_ffn_kernel,
        grid=(t // tm,),
        in_specs=[row(D_MODEL), row(MLA_W), row(RWKV_W), _const_spec(go.shape), _const_spec(gy.shape),
                  _const_spec(wo.shape), _const_spec(g2.shape), _const_spec(wg.shape), _const_spec(wu.shape),
                  _const_spec(wd.shape)],
        out_specs=row(D_MODEL),
        out_shape=jax.ShapeDtypeStruct((t, D_MODEL), f32),
        compiler_params=_params(("parallel",)),
        name="out_ffn",
    )(x1, o, y, go, gy, wo, g2, wg, wu, wd)


def _mla_prep(pm, cos_t, sin_t, gq, gkv, wq, wkv, qn, kn, seq):
    t = pm.shape[0]
    tm = TM_PREP
    n_pos = seq // tm
    row = lambda w: pl.BlockSpec((tm, w), lambda i: (i, 0))
    pos = pl.BlockSpec((tm, LANES), lambda i: (i % n_pos, 0))
    return pl.pallas_call(
        _mla_prep_kernel,
        grid=(t // tm,),
        in_specs=[row(P_MLA_W), pos, pos, _const_spec(gq.shape), _const_spec(gkv.shape), _const_spec(wq.shape),
                  _const_spec(wkv.shape), _const_spec(qn.shape), _const_spec(kn.shape)],
        out_specs=[row(QK_PAD_W), row(QK_PAD_W), row(MLA_W)],
        out_shape=[jax.ShapeDtypeStruct((t, QK_PAD_W), bf16), jax.ShapeDtypeStruct((t, QK_PAD_W), bf16),
                   jax.ShapeDtypeStruct((t, MLA_W), bf16)],
        compiler_params=_params(("parallel",)),
        name="mla_prep",
    )(pm, cos_t, sin_t, gq, gkv, wq, wkv, qn, kn)


def _attention(q, k, v, batch, seq):
    nq = seq // TQ
    return pl.pallas_call(
        _attn_kernel,
        grid=(batch, MLA_HEADS // 2, nq),
        in_specs=[pl.BlockSpec((TQ, 2 * HEAD_PAD), lambda b, hp, i: (b * nq + i, hp)),
                  pl.BlockSpec((seq, 2 * HEAD_PAD), lambda b, hp, i: (b, hp)),
                  pl.BlockSpec((seq, 2 * MLA_V), lambda b, hp, i: (b, hp))],
        out_specs=pl.BlockSpec((TQ, 2 * MLA_V), lambda b, hp, i: (b * nq + i, hp)),
        out_shape=jax.ShapeDtypeStruct((batch * seq, MLA_W), f32),
        compiler_params=_params(("parallel", "parallel", "arbitrary")),
        name="attention",
    )(q, k, v)


def _rwkv(pr, mu, w0, a0, wwa, g2, kk, ka, rk, gng, gnb, e64, tri, ones, batch, seq):
    tm = TM_RWKV
    ns = seq // tm
    vecs = [mu, w0, a0, wwa, g2, kk, ka, rk, gng, gnb, e64, tri, ones]
    return pl.pallas_call(
        _rwkv_kernel,
        grid=(batch, ns),
        in_specs=[pl.BlockSpec((tm, P_RWKV_W), lambda b, i: (b * ns + i, 0))] + [_const_spec(a.shape) for a in vecs],
        out_specs=pl.BlockSpec((tm, RWKV_W), lambda b, i: (b * ns + i, 0)),
        out_shape=jax.ShapeDtypeStruct((batch * seq, RWKV_W), f32),
        scratch_shapes=[pltpu.VMEM((tm + 8, P_RWKV_W), f32),
                        pltpu.VMEM((N_PAIR, LANES, LANES), f32)]
                       + [pltpu.VMEM((tm, RWKV_W), bf16)] * 7
                       + [pltpu.VMEM((tm, RWKV_W), f32)] * 2,
        compiler_params=_params(("arbitrary", "arbitrary")),
        name="rwkv",
    )(pr, *vecs)


def _pad_cols(w, n):
    return jnp.pad(w, ((0, 0), (0, n - w.shape[1])))


def _win_ext(w_in):
    d = w_in.shape[0]
    z = lambda n: jnp.zeros((d, n), w_in.dtype)
    pe = w_in[:, Q_LORA + KV_LORA:MLA_IN]
    half = MLA_ROPE // 2
    rw = w_in[:, MLA_IN:]
    o = 0
    r_w = rw[:, o:o + RWKV_W]; o += RWKV_W
    wl_w = rw[:, o:o + DECAY_LORA]; o += DECAY_LORA
    k_w = rw[:, o:o + RWKV_W]; o += RWKV_W
    v_w = rw[:, o:o + RWKV_W]; o += RWKV_W
    al_w = rw[:, o:o + A_LORA]; o += A_LORA
    gl_w = rw[:, o:o + GATE_LORA]
    return jnp.concatenate([
        w_in[:, :Q_LORA + KV_LORA],
        z(MLA_NOPE), pe, z(LANES - MLA_QK),
        z(MLA_NOPE), pe[:, half:], pe[:, :half], z(LANES - MLA_QK),
        r_w, k_w, v_w, wl_w, al_w, gl_w, z(GATE_PAD - GATE_LORA)], axis=1)


def _mu_ext(mu):
    o = 0
    r_m = mu[o:o + RWKV_W]; o += RWKV_W
    wl_m = mu[o:o + DECAY_LORA]; o += DECAY_LORA
    k_m = mu[o:o + RWKV_W]; o += RWKV_W
    v_m = mu[o:o + RWKV_W]; o += RWKV_W
    al_m = mu[o:o + A_LORA]; o += A_LORA
    gl_m = mu[o:o + GATE_LORA]
    return jnp.concatenate([r_m, k_m, v_m, wl_m, al_m, gl_m, jnp.zeros((GATE_PAD - GATE_LORA,), mu.dtype)])[None, :]


def _head_pad_cols(w, width):
    kdim = w.shape[0]
    w = w.reshape(kdim, MLA_HEADS, width)
    return jnp.pad(w, ((0, 0), (0, 0), (0, HEAD_PAD - width))).reshape(kdim, QK_PAD_W)


def _swap_rope_cols(w):
    kdim = w.shape[0]
    half = MLA_ROPE // 2
    w = w.reshape(kdim, MLA_HEADS, MLA_QK)
    sw = jnp.concatenate([jnp.zeros((kdim, MLA_HEADS, MLA_NOPE), w.dtype), w[..., MLA_NOPE + half:],
                          w[..., MLA_NOPE:MLA_NOPE + half],
                          jnp.zeros((kdim, MLA_HEADS, HEAD_PAD - MLA_QK), w.dtype)], axis=-1)
    return sw.reshape(kdim, QK_PAD_W)


def _norm_pair(g):
    half = MLA_ROPE // 2
    zpad = jnp.zeros((HEAD_PAD - MLA_QK,), g.dtype)
    plain = jnp.concatenate([g, zpad])
    sw = jnp.concatenate([jnp.zeros((MLA_NOPE,), g.dtype), g[MLA_NOPE + half:], g[MLA_NOPE:MLA_NOPE + half], zpad])
    return jnp.stack([plain, sw])


def _rope_tiles(seq):
    pos = jnp.arange(seq, dtype=f32)
    inv_freq = ROPE_BASE ** (-jnp.arange(0, MLA_ROPE, 2, dtype=f32) / MLA_ROPE)
    ang = pos[:, None] * inv_freq[None, :]
    cos, sin = jnp.cos(ang), jnp.sin(ang)
    zpad = jnp.zeros((seq, HEAD_PAD - MLA_QK), f32)
    cos_t = jnp.concatenate([jnp.ones((seq, MLA_NOPE), f32), cos, cos, zpad], axis=1)
    sin_t = jnp.concatenate([jnp.zeros((seq, MLA_NOPE), f32), -sin, sin, zpad], axis=1)
    return cos_t, sin_t


def _block_diag_const(n, blk, lower):
    r = jnp.arange(n)[:, None]
    c = jnp.arange(n)[None, :]
    m = (r // blk) == (c // blk)
    if lower:
        m = m & (c <= r)
    return m.astype(bf16)


@jax.jit
def kernel(x, ffn1_norm, ffn1_w_gate, ffn1_w_up, ffn1_w_down, mix_norm, w_in, q_lat_norm, w_q_up, kv_lat_norm, w_kv_up, q_norm, k_norm, shift_mu, w0, w2, a0, a2, g2, k_k, k_a, r_k, gn_gain, gn_bias, out_norm_mla, out_norm_rwkv, w_out, ffn2_norm, ffn2_w_gate, ffn2_w_up, ffn2_w_down):
    batch, seq, d = x.shape
    xt = x.reshape(batch * seq, d)
    cos_t, sin_t = _rope_tiles(seq)
    e64 = _block_diag_const(RWKV_W, RWKV_HEAD, lower=False)
    tri = _block_diag_const(TM_RWKV, CHUNK, lower=True)
    ones = _block_diag_const(TM_RWKV, CHUNK, lower=False)
    row = lambda a: a[None, :]
    for l in range(DEPTH):
        xt, pm, pr = _ffn_in(xt, row(ffn1_norm[l]), ffn1_w_gate[l].astype(bf16), ffn1_w_up[l].astype(bf16),
                             ffn1_w_down[l].astype(bf16), row(mix_norm[l]), _win_ext(w_in[l]).astype(bf16))
        wq = jnp.concatenate([_head_pad_cols(w_q_up[l], MLA_QK), _swap_rope_cols(w_q_up[l])], axis=1).astype(bf16)
        wkv3 = w_kv_up[l].reshape(KV_LORA, MLA_HEADS, MLA_NOPE + MLA_V)
        wkv = jnp.concatenate([_head_pad_cols(wkv3[..., :MLA_NOPE].reshape(KV_LORA, -1), MLA_NOPE),
                               wkv3[..., MLA_NOPE:].reshape(KV_LORA, MLA_W)], axis=1).astype(bf16)
        q, k, v = _mla_prep(pm, cos_t, sin_t, row(q_lat_norm[l]), row(kv_lat_norm[l]), wq, wkv,
                            _norm_pair(q_norm[l]), _norm_pair(k_norm[l]), seq)
        o = _attention(q, k, v, batch, seq)
        zl = jnp.zeros((DECAY_LORA, RWKV_W), f32)
        wwa = jnp.concatenate([jnp.concatenate([w2[l], zl], axis=1), jnp.concatenate([zl, a2[l]], axis=1)], axis=0)
        g2p = jnp.pad(g2[l], ((0, GATE_PAD - GATE_LORA), (0, 0)))
        y = _rwkv(pr, _mu_ext(shift_mu[l]), row(w0[l]), row(a0[l]), wwa.astype(bf16), g2p.astype(bf16),
                  row(k_k[l]), row(k_a[l]), row(r_k[l].reshape(-1)), row(gn_gain[l]), row(gn_bias[l]),
                  e64, tri, ones, batch, seq)
        xt = _out_ffn(xt, o, y, row(out_norm_mla[l]), row(out_norm_rwkv[l]), w_out[l].astype(bf16),
                      row(ffn2_norm[l]), ffn2_w_gate[l].astype(bf16), ffn2_w_up[l].astype(bf16),
                      ffn2_w_down[l].astype(bf16))
    return xt.reshape(batch, seq, d)
```

```python
import functools

import jax
import jax.numpy as jnp
from jax import lax
from jax.experimental import pallas as pl
from jax.experimental.pallas import tpu as pltpu

f32 = jnp.float32
bf16 = jnp.bfloat16

D_MODEL = 1024
DEPTH = 4
CHUNK = 64
D_FF = 2816
EPS = 1e-6
MLA_HEADS = 8
MLA_NOPE = 64
MLA_ROPE = 32
MLA_QK = MLA_NOPE + MLA_ROPE
MLA_V = 64
Q_LORA = 256
KV_LORA = 128
ROPE_BASE = 10000.0
RWKV_HEADS = 8
RWKV_HEAD = 64
RWKV_W = RWKV_HEADS * RWKV_HEAD
DECAY_LORA = 64
A_LORA = 64
GATE_LORA = 160
GN_EPS = 64e-5
MLA_W = MLA_HEADS * MLA_V
MLA_IN = Q_LORA + KV_LORA + MLA_ROPE

LANES = 128
HEAD_PAD = LANES
QK_PAD_W = MLA_HEADS * HEAD_PAD
P_MLA_W = Q_LORA + KV_LORA + 2 * LANES
GATE_PAD = 2 * LANES
P_RWKV_W = 3 * RWKV_W + LANES + GATE_PAD
VMEM_LIMIT = 56 * 1024 * 1024

TM_FFN = 256
TM_PREP = 512
TQ = 512
TM_RWKV = 256
N_PAIR = RWKV_HEADS // 2


def _rms(x, g):
    return x * lax.rsqrt(jnp.mean(x * x, axis=-1, keepdims=True) + EPS) * g


def _dot(a, b):
    return jnp.dot(a, b, preferred_element_type=f32)


def _dot_nt(a, b):
    return lax.dot_general(a, b, (((1,), (1,)), ((), ())), preferred_element_type=f32)


def _dot_tn(a, b):
    return lax.dot_general(a, b, (((0,), (0,)), ((), ())), preferred_element_type=f32)


def _dot_split(c, x):
    hi = x.astype(bf16)
    lo = (x - hi.astype(f32)).astype(bf16)
    return _dot(c, hi) + _dot(c, lo)


def _dot_split_r(x, c):
    hi = x.astype(bf16)
    lo = (x - hi.astype(f32)).astype(bf16)
    return _dot(hi, c) + _dot(lo, c)


def _swiglu_half(x, g, wg_ref, wu_ref, wd_ref):
    h = _rms(x, g).astype(bf16)
    gate = _dot(h, wg_ref[...])
    up = _dot(h, wu_ref[...])
    act = (jax.nn.silu(gate) * up).astype(bf16)
    return x + 0.5 * _dot(act, wd_ref[...])


def _ffn_in_kernel(x_ref, g1_ref, wg_ref, wu_ref, wd_ref, gm_ref, win_ref, x1_ref, pm_ref, pr_ref):
    x1 = _swiglu_half(x_ref[...], g1_ref[...], wg_ref, wu_ref, wd_ref)
    x1_ref[...] = x1
    p = _dot(_rms(x1, gm_ref[...]).astype(bf16), win_ref[...])
    pm_ref[...] = p[:, :P_MLA_W]
    pr_ref[...] = p[:, P_MLA_W:]


def _out_ffn_kernel(x_ref, o_ref, y_ref, go_ref, gy_ref, wo_ref, g2_ref, wg_ref, wu_ref, wd_ref, x3_ref):
    ho = _rms(o_ref[...], go_ref[...]).astype(bf16)
    hy = _rms(y_ref[...], gy_ref[...]).astype(bf16)
    x2 = x_ref[...] + _dot(ho, wo_ref[:MLA_W, :]) + _dot(hy, wo_ref[MLA_W:, :])
    x3_ref[...] = _swiglu_half(x2, g2_ref[...], wg_ref, wu_ref, wd_ref)


def _mla_prep_kernel(pm_ref, cos_ref, sin_ref, gq_ref, gkv_ref, wq_ref, wk_ref, wvt_ref, qn_ref, kn_ref,
                     q_ref, k_ref, vt_ref):
    pm = pm_ref[...]
    q_lat = pm[:, :Q_LORA]
    kv_lat = pm[:, Q_LORA:Q_LORA + KV_LORA]
    kpe = pm[:, Q_LORA + KV_LORA:Q_LORA + KV_LORA + LANES]
    kpe_sw = pm[:, Q_LORA + KV_LORA + LANES:]
    cos_t = cos_ref[...]
    sin_t = sin_ref[...]
    q2 = _dot(_rms(q_lat, gq_ref[...]).astype(bf16), wq_ref[...])
    hkv = _rms(kv_lat, gkv_ref[...]).astype(bf16)
    kv = _dot(hkv, wk_ref[...])
    vt_ref[...] = _dot_nt(wvt_ref[...], hkv).astype(bf16)
    qc = qn_ref[0:1, :] * cos_t
    qs = qn_ref[1:2, :] * sin_t
    kc = kn_ref[0:1, :] * cos_t
    ks = kn_ref[1:2, :] * sin_t
    scale = MLA_QK ** -0.5
    for h in range(MLA_HEADS):
        sl = slice(h * HEAD_PAD, (h + 1) * HEAD_PAD)
        qh = q2[:, sl]
        qsw = q2[:, QK_PAD_W + h * HEAD_PAD:QK_PAD_W + (h + 1) * HEAD_PAD]
        rs = lax.rsqrt(jnp.sum(qh * qh, axis=-1, keepdims=True) * (1.0 / MLA_QK) + EPS)
        q_ref[:, sl] = ((qh * qc + qsw * qs) * (rs * scale)).astype(bf16)
        kh = kv[:, sl] + kpe
        rk = lax.rsqrt(jnp.sum(kh * kh, axis=-1, keepdims=True) * (1.0 / MLA_QK) + EPS)
        k_ref[:, sl] = ((kh * kc + kpe_sw * ks) * rk).astype(bf16)


def _attn_kernel(q_ref, k_ref, vt_ref, o_ref):
    i = pl.program_id(2)
    key_chunk = lax.broadcasted_iota(jnp.int32, (TQ, TQ), 0) // CHUNK
    qry_chunk = lax.broadcasted_iota(jnp.int32, (TQ, TQ), 1) // CHUNK
    diag_mask = key_chunk <= qry_chunk

    def block(j, carry, masked):
        start = pl.multiple_of(j * TQ, TQ)
        new = []
        heads = [slice(h * HEAD_PAD, (h + 1) * HEAD_PAD) for h in range(2)]
        sts = [_dot_nt(k_ref[pl.ds(start, TQ), head], q_ref[:, head]) for head in heads]
        for h in range(2):
            m, l, acc = carry[h]
            st = sts[h]
            if masked:
                st = jnp.where(diag_mask, st, -jnp.inf)
            m_new = jnp.maximum(m, jnp.max(st, axis=0, keepdims=True))
            alpha = jnp.exp(m - m_new)
            pr = jnp.exp(st - m_new)
            l = alpha * l + jnp.sum(pr, axis=0, keepdims=True)
            acc = alpha * acc + _dot(vt_ref[h * MLA_V:(h + 1) * MLA_V, pl.ds(start, TQ)], pr.astype(bf16))
            new.append((m_new, l, acc))
        return tuple(new)

    init = tuple((jnp.full((1, TQ), -jnp.inf, f32), jnp.zeros((1, TQ), f32), jnp.zeros((MLA_V, TQ), f32))
                 for _ in range(2))
    carry = lax.fori_loop(0, i, functools.partial(block, masked=False), init)
    (_, l0, acc0), (_, l1, acc1) = block(i, carry, True)
    o_ref[...] = jnp.concatenate([acc0 / l0, acc1 / l1], axis=0).T


def _rwkv_kernel(p_ref, mu_ref, w0_ref, a0_ref, wwa_ref, g2_ref, kk_ref, ka_ref, rk_ref, gng_ref, gnb_ref,
                 e64_ref, tri_ref, ones_ref, y_ref,
                 shift_scr, state_scr, at_scr, rt_scr, bt_scr, kt_scr, bp_scr, kp_scr, v_scr, pc_scr):
    tm = TM_RWKV
    i = pl.program_id(1)

    @pl.when(i == 0)
    def _():
        shift_scr[0:8, :] = jnp.zeros((8, P_RWKV_W), f32)
        state_scr[...] = jnp.zeros_like(state_scr)

    cur = p_ref[...]
    shift_scr[8:8 + tm, :] = cur
    prev = shift_scr[7:7 + tm, :]
    xs = cur + mu_ref[...] * (prev - cur)
    shift_scr[0:8, :] = shift_scr[tm:tm + 8, :]

    r = xs[:, 0:RWKV_W]
    k = xs[:, RWKV_W:2 * RWKV_W]
    v = xs[:, 2 * RWKV_W:3 * RWKV_W]
    wa = xs[:, 3 * RWKV_W:3 * RWKV_W + LANES]
    gl = xs[:, 3 * RWKV_W + LANES:]

    lane = lax.broadcasted_iota(jnp.int32, (tm, LANES), 1)
    wa = jnp.where(lane < DECAY_LORA, jnp.tanh(wa), wa)
    wa_pre = _dot(wa.astype(bf16), wwa_ref[...])
    z = -(w0_ref[...] + wa_pre[:, :RWKV_W])
    softplus = jnp.maximum(z, 0.0) + jnp.log(1.0 + jnp.exp(-jnp.abs(z)))
    logd = -jnp.exp(-softplus - 0.5)
    a = jax.nn.sigmoid(a0_ref[...] + wa_pre[:, RWKV_W:])
    g = _dot(jax.nn.sigmoid(gl).astype(bf16), g2_ref[...])

    e64 = e64_ref[...]
    kk = k * kk_ref[...]
    kk = kk / jnp.maximum(jnp.sqrt(_dot_split_r(kk * kk, e64)), 1e-12)
    kmod = k * (1.0 + (a - 1.0) * ka_ref[...])
    bonus = _dot_split_r(r * kmod * rk_ref[...], e64) * v

    cl = _dot_split(tri_ref[...], logd)
    cl_end = _dot_split(ones_ref[...], logd)
    at_scr[...] = (-kk * jnp.exp(cl - logd)).astype(bf16)
    rt_scr[...] = (r * jnp.exp(cl)).astype(bf16)
    pinv = jnp.exp(-cl)
    bt_scr[...] = (kk * a * pinv).astype(bf16)
    kt_scr[...] = (kmod * pinv).astype(bf16)
    q_end = jnp.exp(cl_end - cl)
    bp_scr[...] = (kk * a * q_end).astype(bf16)
    kp_scr[...] = (kmod * q_end).astype(bf16)
    v_scr[...] = v.astype(bf16)
    pc_scr[...] = jnp.exp(cl_end)

    c2 = 2 * CHUNK
    lane_c = lax.broadcasted_iota(jnp.int32, (CHUNK, c2), 1)
    row_c = lax.broadcasted_iota(jnp.int32, (CHUNK, c2), 0)
    left = lane_c < CHUNK
    s_idx = jnp.where(left, lane_c, lane_c - CHUNK)
    strict = s_idx < row_c
    incl = s_idx <= row_c
    eye_sbs = (s_idx == row_c).astype(f32)
    level_masks = [((row_c >> (lvl + 1)) == (s_idx >> (lvl + 1))) & (((row_c >> lvl) & 1) == 1)
                   & (((s_idx >> lvl) & 1) == 0) for lvl in range(6)]
    bd_mask = ((lax.broadcasted_iota(jnp.int32, (c2, c2), 0) // CHUNK)
               == (lax.broadcasted_iota(jnp.int32, (c2, c2), 1) // CHUNK))

    def stack2(x):
        zero = jnp.zeros_like(x)
        return jnp.concatenate([jnp.where(left, x, zero), jnp.where(left, zero, x)], axis=0)

    n_chunk = tm // CHUNK
    units = [(c, p) for c in range(n_chunk) for p in range(N_PAIR)]

    def region(c, p):
        return slice(c * CHUNK, (c + 1) * CHUNK), slice(p * LANES, (p + 1) * LANES)

    ar, vb, vst, a_ab, a_ak, a_r, t_inv, w1 = {}, {}, {}, {}, {}, {}, {}, {}
    for u_ in units:
        rows, cols = region(*u_)
        ar[u_] = jnp.concatenate([at_scr[rows, cols], rt_scr[rows, cols]], axis=0)
        yk = jnp.concatenate([stack2(bt_scr[rows, cols]), stack2(kt_scr[rows, cols])], axis=0)
        gm = _dot_nt(ar[u_], yk)
        a_ab[u_] = gm[:CHUNK, :c2]
        a_ak[u_] = jnp.where(strict, gm[:CHUNK, c2:], 0.0).astype(bf16)
        a_r[u_] = jnp.concatenate([jnp.where(incl, gm[CHUNK:, :c2], 0.0),
                                   jnp.where(incl, gm[CHUNK:, c2:], 0.0)], axis=1).astype(bf16)
        t_inv[u_] = eye_sbs + jnp.where(level_masks[0], a_ab[u_], 0.0)
    for lvl in range(1, 6):
        nt = {}
        for u_ in units:
            n_k = jnp.where(level_masks[lvl], a_ab[u_], 0.0).astype(bf16)
            nt[u_] = _dot(n_k, stack2(t_inv[u_].astype(bf16)))
        for u_ in units:
            t_inv[u_] = t_inv[u_] + _dot(t_inv[u_].astype(bf16), stack2(nt[u_].astype(bf16)))
    for u_ in units:
        rows, cols = region(*u_)
        vb[u_] = v_scr[rows, cols]
        vst[u_] = stack2(vb[u_])
        w1[u_] = _dot(a_ak[u_], vst[u_])
        t_inv[u_] = t_inv[u_].astype(bf16)

    ys = [[None] * N_PAIR for _ in range(n_chunk)]
    ht = [state_scr[p] for p in range(N_PAIR)]
    for c in range(n_chunk):
        arh, ub = [None] * N_PAIR, [None] * N_PAIR
        for p in range(N_PAIR):
            arh[p] = _dot_nt(ar[c, p], ht[p].astype(bf16))
        for p in range(N_PAIR):
            ub[p] = _dot(t_inv[c, p], stack2((arh[p][:CHUNK] + w1[c, p]).astype(bf16))).astype(bf16)
        for p in range(N_PAIR):
            rows, cols = region(c, p)
            upd = _dot_tn(jnp.concatenate([ub[p], vb[c, p]], axis=0),
                          jnp.concatenate([bp_scr[rows, cols], kp_scr[rows, cols]], axis=0))
            ht[p] = ht[p] * pc_scr[c * CHUNK:c * CHUNK + 1, cols] + jnp.where(bd_mask, upd, 0.0)
        for p in range(N_PAIR):
            ys[c][p] = arh[p][CHUNK:] + _dot(a_r[c, p], jnp.concatenate([stack2(ub[p]), vst[c, p]], axis=0))
    for p in range(N_PAIR):
        state_scr[p] = ht[p]

    y = jnp.concatenate([jnp.concatenate(row_y, axis=1) for row_y in ys], axis=0)
    inv_n = 1.0 / RWKV_HEAD
    mean = _dot_split_r(y, e64) * inv_n
    yc = y - mean
    var = _dot_split_r(yc * yc, e64) * inv_n
    yn = yc * lax.rsqrt(var + GN_EPS) * gng_ref[...] + gnb_ref[...]
    y_ref[...] = (yn + bonus) * g


def _const_spec(shape):
    return pl.BlockSpec(shape, lambda *_: (0,) * len(shape), pipeline_mode=pl.Buffered(1))


def _params(sem):
    return pltpu.CompilerParams(dimension_semantics=sem, vmem_limit_bytes=VMEM_LIMIT)


def _ffn_in(x, g1, wg, wu, wd, gm, win):
    t = x.shape[0]
    tm = TM_FFN
    row = lambda w: pl.BlockSpec((tm, w), lambda i: (i, 0))
    return pl.pallas_call(
        _ffn_in_kernel,
        grid=(t // tm,),
        in_specs=[row(D_MODEL), _const_spec(g1.shape), _const_spec(wg.shape), _const_spec(wu.shape),
                  _const_spec(wd.shape), _const_spec(gm.shape), _const_spec(win.shape)],
        out_specs=[row(D_MODEL), row(P_MLA_W), row(P_RWKV_W)],
        out_shape=[jax.ShapeDtypeStruct((t, D_MODEL), f32), jax.ShapeDtypeStruct((t, P_MLA_W), f32),
                   jax.ShapeDtypeStruct((t, P_RWKV_W), f32)],
        compiler_params=_params(("parallel",)),
        name="ffn_in",
    )(x, g1, wg, wu, wd, gm, win)


def _out_ffn(x1, o, y, go, gy, wo, g2, wg, wu, wd):
    t = x1.shape[0]
    tm = TM_FFN
    row = lambda w: pl.BlockSpec((tm, w), lambda i: (i, 0))
    return pl.pallas_call(
        _out_ffn_kernel,
        grid=(t // tm,),
        in_specs=[row(D_MODEL), row(MLA_W), row(RWKV_W), _const_spec(go.shape), _const_spec(gy.shape),
                  _const_spec(wo.shape), _const_spec(g2.shape), _const_spec(wg.shape), _const_spec(wu.shape),
                  _const_spec(wd.shape)],
        out_specs=row(D_MODEL),
        out_shape=jax.ShapeDtypeStruct((t, D_MODEL), f32),
        compiler_params=_params(("parallel",)),
        name="out_ffn",
    )(x1, o, y, go, gy, wo, g2, wg, wu, wd)


def _mla_prep(pm, cos_t, sin_t, gq, gkv, wq, wk, wvt, qn, kn, seq):
    t = pm.shape[0]
    tm = TM_PREP
    n_pos = seq // tm
    row = lambda w: pl.BlockSpec((tm, w), lambda i: (i, 0))
    pos = pl.BlockSpec((tm, LANES), lambda i: (i % n_pos, 0))
    return pl.pallas_call(
        _mla_prep_kernel,
        grid=(t // tm,),
        in_specs=[row(P_MLA_W), pos, pos, _const_spec(gq.shape), _const_spec(gkv.shape), _const_spec(wq.shape),
                  _const_spec(wk.shape), _const_spec(wvt.shape), _const_spec(qn.shape), _const_spec(kn.shape)],
        out_specs=[row(QK_PAD_W), row(QK_PAD_W), pl.BlockSpec((MLA_W, tm), lambda i: (0, i))],
        out_shape=[jax.ShapeDtypeStruct((t, QK_PAD_W), bf16), jax.ShapeDtypeStruct((t, QK_PAD_W), bf16),
                   jax.ShapeDtypeStruct((MLA_W, t), bf16)],
        compiler_params=_params(("parallel",)),
        name="mla_prep",
    )(pm, cos_t, sin_t, gq, gkv, wq, wk, wvt, qn, kn)


def _attention(q, k, vt, batch, seq):
    nq = seq // TQ
    return pl.pallas_call(
        _attn_kernel,
        grid=(batch, MLA_HEADS // 2, nq),
        in_specs=[pl.BlockSpec((TQ, 2 * HEAD_PAD), lambda b, hp, i: (b * nq + i, hp)),
                  pl.BlockSpec((seq, 2 * HEAD_PAD), lambda b, hp, i: (b, hp)),
                  pl.BlockSpec((2 * MLA_V, seq), lambda b, hp, i: (hp, b))],
        out_specs=pl.BlockSpec((TQ, 2 * MLA_V), lambda b, hp, i: (b * nq + i, hp)),
        out_shape=jax.ShapeDtypeStruct((batch * seq, MLA_W), f32),
        compiler_params=_params(("parallel", "parallel", "arbitrary")),
        name="attention",
    )(q, k, vt)


def _rwkv(pr, mu, w0, a0, wwa, g2, kk, ka, rk, gng, gnb, e64, tri, ones, batch, seq):
    tm = TM_RWKV
    ns = seq // tm
    vecs = [mu, w0, a0, wwa, g2, kk, ka, rk, gng, gnb, e64, tri, ones]
    return pl.pallas_call(
        _rwkv_kernel,
        grid=(batch, ns),
        in_specs=[pl.BlockSpec((tm, P_RWKV_W), lambda b, i: (b * ns + i, 0))] + [_const_spec(a.shape) for a in vecs],
        out_specs=pl.BlockSpec((tm, RWKV_W), lambda b, i: (b * ns + i, 0)),
        out_shape=jax.ShapeDtypeStruct((batch * seq, RWKV_W), f32),
        scratch_shapes=[pltpu.VMEM((tm + 8, P_RWKV_W), f32),
                        pltpu.VMEM((N_PAIR, LANES, LANES), f32)]
                       + [pltpu.VMEM((tm, RWKV_W), bf16)] * 7
                       + [pltpu.VMEM((tm, RWKV_W), f32)],
        compiler_params=_params(("arbitrary", "arbitrary")),
        name="rwkv",
    )(pr, *vecs)


def _pad_cols(w, n):
    return jnp.pad(w, ((0, 0), (0, n - w.shape[1])))


def _win_ext(w_in):
    d = w_in.shape[0]
    z = lambda n: jnp.zeros((d, n), w_in.dtype)
    pe = w_in[:, Q_LORA + KV_LORA:MLA_IN]
    half = MLA_ROPE // 2
    rw = w_in[:, MLA_IN:]
    o = 0
    r_w = rw[:, o:o + RWKV_W]; o += RWKV_W
    wl_w = rw[:, o:o + DECAY_LORA]; o += DECAY_LORA
    k_w = rw[:, o:o + RWKV_W]; o += RWKV_W
    v_w = rw[:, o:o + RWKV_W]; o += RWKV_W
    al_w = rw[:, o:o + A_LORA]; o += A_LORA
    gl_w = rw[:, o:o + GATE_LORA]
    return jnp.concatenate([
        w_in[:, :Q_LORA + KV_LORA],
        z(MLA_NOPE), pe, z(LANES - MLA_QK),
        z(MLA_NOPE), pe[:, half:], pe[:, :half], z(LANES - MLA_QK),
        r_w, k_w, v_w, wl_w, al_w, gl_w, z(GATE_PAD - GATE_LORA)], axis=1)


def _mu_ext(mu):
    o = 0
    r_m = mu[o:o + RWKV_W]; o += RWKV_W
    wl_m = mu[o:o + DECAY_LORA]; o += DECAY_LORA
    k_m = mu[o:o + RWKV_W]; o += RWKV_W
    v_m = mu[o:o + RWKV_W]; o += RWKV_W
    al_m = mu[o:o + A_LORA]; o += A_LORA
    gl_m = mu[o:o + GATE_LORA]
    return jnp.concatenate([r_m, k_m, v_m, wl_m, al_m, gl_m, jnp.zeros((GATE_PAD - GATE_LORA,), mu.dtype)])[None, :]


def _head_pad_cols(w, width):
    kdim = w.shape[0]
    w = w.reshape(kdim, MLA_HEADS, width)
    return jnp.pad(w, ((0, 0), (0, 0), (0, HEAD_PAD - width))).reshape(kdim, QK_PAD_W)


def _swap_rope_cols(w):
    kdim = w.shape[0]
    half = MLA_ROPE // 2
    w = w.reshape(kdim, MLA_HEADS, MLA_QK)
    sw = jnp.concatenate([jnp.zeros((kdim, MLA_HEADS, MLA_NOPE), w.dtype), w[..., MLA_NOPE + half:],
                          w[..., MLA_NOPE:MLA_NOPE + half],
                          jnp.zeros((kdim, MLA_HEADS, HEAD_PAD - MLA_QK), w.dtype)], axis=-1)
    return sw.reshape(kdim, QK_PAD_W)


def _norm_pair(g):
    half = MLA_ROPE // 2
    zpad = jnp.zeros((HEAD_PAD - MLA_QK,), g.dtype)
    plain = jnp.concatenate([g, zpad])
    sw = jnp.concatenate([jnp.zeros((MLA_NOPE,), g.dtype), g[MLA_NOPE + half:], g[MLA_NOPE:MLA_NOPE + half], zpad])
    return jnp.stack([plain, sw])


def _rope_tiles(seq):
    pos = jnp.arange(seq, dtype=f32)
    inv_freq = ROPE_BASE ** (-jnp.arange(0, MLA_ROPE, 2, dtype=f32) / MLA_ROPE)
    ang = pos[:, None] * inv_freq[None, :]
    cos, sin = jnp.cos(ang), jnp.sin(ang)
    zpad = jnp.zeros((seq, HEAD_PAD - MLA_QK), f32)
    cos_t = jnp.concatenate([jnp.ones((seq, MLA_NOPE), f32), cos, cos, zpad], axis=1)
    sin_t = jnp.concatenate([jnp.zeros((seq, MLA_NOPE), f32), -sin, sin, zpad], axis=1)
    return cos_t, sin_t


def _block_diag_const(n, blk, lower):
    r = jnp.arange(n)[:, None]
    c = jnp.arange(n)[None, :]
    m = (r // blk) == (c // blk)
    if lower:
        m = m & (c <= r)
    return m.astype(bf16)


@jax.jit
def kernel(x, ffn1_norm, ffn1_w_gate, ffn1_w_up, ffn1_w_down, mix_norm, w_in, q_lat_norm, w_q_up, kv_lat_norm, w_kv_up, q_norm, k_norm, shift_mu, w0, w2, a0, a2, g2, k_k, k_a, r_k, gn_gain, gn_bias, out_norm_mla, out_norm_rwkv, w_out, ffn2_norm, ffn2_w_gate, ffn2_w_up, ffn2_w_down):
    batch, seq, d = x.shape
    xt = x.reshape(batch * seq, d)
    cos_t, sin_t = _rope_tiles(seq)
    e64 = _block_diag_const(RWKV_W, RWKV_HEAD, lower=False)
    tri = _block_diag_const(TM_RWKV, CHUNK, lower=True)
    ones = _block_diag_const(TM_RWKV, CHUNK, lower=False)
    row = lambda a: a[None, :]
    for l in range(DEPTH):
        xt, pm, pr = _ffn_in(xt, row(ffn1_norm[l]), ffn1_w_gate[l].astype(bf16), ffn1_w_up[l].astype(bf16),
                             ffn1_w_down[l].astype(bf16), row(mix_norm[l]), _win_ext(w_in[l]).astype(bf16))
        wq = jnp.concatenate([_head_pad_cols(w_q_up[l], MLA_QK), _swap_rope_cols(w_q_up[l])], axis=1).astype(bf16)
        wkv3 = w_kv_up[l].reshape(KV_LORA, MLA_HEADS, MLA_NOPE + MLA_V)
        wk = _head_pad_cols(wkv3[..., :MLA_NOPE].reshape(KV_LORA, -1), MLA_NOPE).astype(bf16)
        wvt = wkv3[..., MLA_NOPE:].reshape(KV_LORA, MLA_W).T.astype(bf16)
        q, k, vt = _mla_prep(pm, cos_t, sin_t, row(q_lat_norm[l]), row(kv_lat_norm[l]), wq, wk, wvt,
                             _norm_pair(q_norm[l]), _norm_pair(k_norm[l]), seq)
        o = _attention(q, k, vt, batch, seq)
        zl = jnp.zeros((DECAY_LORA, RWKV_W), f32)
        wwa = jnp.concatenate([jnp.concatenate([w2[l], zl], axis=1), jnp.concatenate([zl, a2[l]], axis=1)], axis=0)
        g2p = jnp.pad(g2[l], ((0, GATE_PAD - GATE_LORA), (0, 0)))
        y = _rwkv(pr, _mu_ext(shift_mu[l]), row(w0[l]), row(a0[l]), wwa.astype(bf16), g2p.astype(bf16),
                  row(k_k[l]), row(k_a[l]), row(r_k[l].reshape(-1)), row(gn_gain[l]), row(gn_bias[l]),
                  e64, tri, ones, batch, seq)
        xt = _out_ffn(xt, o, y, row(out_norm_mla[l]), row(out_norm_rwkv[l]), w_out[l].astype(bf16),
                      row(ffn2_norm[l]), ffn2_w_gate[l].astype(bf16), ffn2_w_up[l].astype(bf16),
                      ffn2_w_down[l].astype(bf16))
    return xt.reshape(batch, seq, d)
```

```python
import functools

import jax
import jax.numpy as jnp
from jax import lax
from jax.experimental import pallas as pl
from jax.experimental.pallas import tpu as pltpu

f32 = jnp.float32
bf16 = jnp.bfloat16

D_MODEL = 1024
DEPTH = 4
CHUNK = 64
D_FF = 2816
EPS = 1e-6
MLA_HEADS = 8
MLA_NOPE = 64
MLA_ROPE = 32
MLA_QK = MLA_NOPE + MLA_ROPE
MLA_V = 64
Q_LORA = 256
KV_LORA = 128
ROPE_BASE = 10000.0
RWKV_HEADS = 8
RWKV_HEAD = 64
RWKV_W = RWKV_HEADS * RWKV_HEAD
DECAY_LORA = 64
A_LORA = 64
GATE_LORA = 160
GN_EPS = 64e-5
MLA_W = MLA_HEADS * MLA_V
MLA_IN = Q_LORA + KV_LORA + MLA_ROPE

LANES = 128
HEAD_PAD = LANES
QK_PAD_W = MLA_HEADS * HEAD_PAD
P_MLA_W = Q_LORA + KV_LORA + 2 * LANES
GATE_PAD = 2 * LANES
P_RWKV_W = 3 * RWKV_W + LANES + GATE_PAD
VMEM_LIMIT = 56 * 1024 * 1024

TM_FFN = 256
TM_PREP = 512
TQ = 512
ONES_ROWS = 16
LOG2_E = 1.4426950408889634
TM_RWKV = 256
N_PAIR = RWKV_HEADS // 2


def _rms(x, g):
    return x * lax.rsqrt(jnp.mean(x * x, axis=-1, keepdims=True) + EPS) * g


def _dot(a, b):
    return jnp.dot(a, b, preferred_element_type=f32)


def _dot_nt(a, b):
    return lax.dot_general(a, b, (((1,), (1,)), ((), ())), preferred_element_type=f32)


def _dot_tn(a, b):
    return lax.dot_general(a, b, (((0,), (0,)), ((), ())), preferred_element_type=f32)


def _dot_split(c, x):
    hi = x.astype(bf16)
    lo = (x - hi.astype(f32)).astype(bf16)
    return _dot(c, hi) + _dot(c, lo)


def _dot_split_r(x, c):
    hi = x.astype(bf16)
    lo = (x - hi.astype(f32)).astype(bf16)
    return _dot(hi, c) + _dot(lo, c)


def _swiglu_half(x, g, wg_ref, wu_ref, wd_ref):
    h = _rms(x, g).astype(bf16)
    gate = _dot(h, wg_ref[...])
    up = _dot(h, wu_ref[...])
    act = (jax.nn.silu(gate) * up).astype(bf16)
    return x + 0.5 * _dot(act, wd_ref[...])


def _ffn_in_kernel(x_ref, g1_ref, wg_ref, wu_ref, wd_ref, gm_ref, win_ref, x1_ref, pm_ref, pr_ref):
    x1 = _swiglu_half(x_ref[...], g1_ref[...], wg_ref, wu_ref, wd_ref)
    x1_ref[...] = x1
    p = _dot(_rms(x1, gm_ref[...]).astype(bf16), win_ref[...])
    pm_ref[...] = p[:, :P_MLA_W]
    pr_ref[...] = p[:, P_MLA_W:]


def _out_ffn_kernel(x_ref, o_ref, y_ref, go_ref, gy_ref, wo_ref, g2_ref, wg_ref, wu_ref, wd_ref, x3_ref):
    ho = _rms(o_ref[...], go_ref[...]).astype(bf16)
    hy = _rms(y_ref[...], gy_ref[...]).astype(bf16)
    x2 = x_ref[...] + _dot(ho, wo_ref[:MLA_W, :]) + _dot(hy, wo_ref[MLA_W:, :])
    x3_ref[...] = _swiglu_half(x2, g2_ref[...], wg_ref, wu_ref, wd_ref)


def _mla_prep_kernel(pm_ref, cos_ref, sin_ref, gq_ref, gkv_ref, wq_ref, wk_ref, wvt_ref, qn_ref, kn_ref,
                     q_ref, k_ref, vt_ref):
    pm = pm_ref[...]
    q_lat = pm[:, :Q_LORA]
    kv_lat = pm[:, Q_LORA:Q_LORA + KV_LORA]
    kpe = pm[:, Q_LORA + KV_LORA:Q_LORA + KV_LORA + LANES]
    kpe_sw = pm[:, Q_LORA + KV_LORA + LANES:]
    cos_t = cos_ref[...]
    sin_t = sin_ref[...]
    q2 = _dot(_rms(q_lat, gq_ref[...]).astype(bf16), wq_ref[...])
    hkv = _rms(kv_lat, gkv_ref[...]).astype(bf16)
    kv = _dot(hkv, wk_ref[...])
    vt_ref[...] = _dot_nt(wvt_ref[...], hkv).astype(bf16)
    qc = qn_ref[0:1, :] * cos_t
    qs = qn_ref[1:2, :] * sin_t
    kc = kn_ref[0:1, :] * cos_t
    ks = kn_ref[1:2, :] * sin_t
    scale = MLA_QK ** -0.5 * LOG2_E
    for h in range(MLA_HEADS):
        sl = slice(h * HEAD_PAD, (h + 1) * HEAD_PAD)
        qh = q2[:, sl]
        qsw = q2[:, QK_PAD_W + h * HEAD_PAD:QK_PAD_W + (h + 1) * HEAD_PAD]
        rs = lax.rsqrt(jnp.sum(qh * qh, axis=-1, keepdims=True) * (1.0 / MLA_QK) + EPS)
        q_ref[:, sl] = ((qh * qc + qsw * qs) * (rs * scale)).astype(bf16)
        kh = kv[:, sl] + kpe
        rk = lax.rsqrt(jnp.sum(kh * kh, axis=-1, keepdims=True) * (1.0 / MLA_QK) + EPS)
        k_ref[:, sl] = ((kh * kc + kpe_sw * ks) * rk).astype(bf16)


def _attn_kernel(q_ref, k_ref, vt_ref, o_ref, s0_scr, s1_scr):
    i = pl.program_id(2)
    chunk_gap = (lax.broadcasted_iota(jnp.int32, (TQ, TQ), 0) // CHUNK
                 - lax.broadcasted_iota(jnp.int32, (TQ, TQ), 1) // CHUNK)
    heads = [slice(h * HEAD_PAD, (h + 1) * HEAD_PAD) for h in range(2)]
    ones_rows = jnp.ones((ONES_ROWS, TQ), bf16)

    def scores(j, s_scr):
        start = pl.multiple_of(j * TQ, TQ)
        for h in range(2):
            s_scr[h] = _dot_nt(k_ref[pl.ds(start, TQ), heads[h]], q_ref[:, heads[h]])

    def consume(j, s_scr, stats, masked):
        start = pl.multiple_of(j * TQ, TQ)
        new = []
        for h in range(2):
            m, acc = stats[h]
            st = s_scr[h]
            if masked:
                st = jnp.where(chunk_gap <= 0, st, -jnp.inf)
            m_new = jnp.maximum(m, jnp.max(st, axis=0, keepdims=True))
            alpha = jnp.exp2(m - m_new)
            pr = jnp.exp2(st - m_new).astype(bf16)
            vt_aug = jnp.concatenate([vt_ref[h * MLA_V:(h + 1) * MLA_V, pl.ds(start, TQ)], ones_rows], axis=0)
            new.append((m_new, alpha * acc + _dot(vt_aug, pr)))
        return tuple(new)

    def pair(jj, stats):
        scores(2 * jj + 1, s1_scr)
        stats = consume(2 * jj, s0_scr, stats, False)
        scores(2 * jj + 2, s0_scr)
        return consume(2 * jj + 1, s1_scr, stats, False)

    init = tuple((jnp.full((1, TQ), -jnp.inf, f32), jnp.zeros((MLA_V + ONES_ROWS, TQ), f32)) for _ in range(2))
    scores(0, s0_scr)
    n_full = i // 2
    stats = lax.fori_loop(0, n_full, pair, init)
    def odd_tail(stats):
        scores(i, s1_scr)
        return consume(i, s1_scr, consume(i - 1, s0_scr, stats, False), True)

    def even_tail(stats):
        return consume(i, s0_scr, stats, True)

    (_, acc0), (_, acc1) = lax.cond(i % 2 == 1, odd_tail, even_tail, stats)
    o_ref[...] = jnp.concatenate([acc0[:MLA_V] / acc0[MLA_V:MLA_V + 1], acc1[:MLA_V] / acc1[MLA_V:MLA_V + 1]],
                                 axis=0).T


def _rwkv_kernel(p_ref, mu_ref, w0_ref, a0_ref, wwa_ref, g2_ref, kk_ref, ka_ref, rk_ref, gng_ref, gnb_ref,
                 e64_ref, tri_ref, y_ref,
                 shift_scr, state_scr, at_scr, rt_scr, bt_scr, kt_scr, bp_scr, kp_scr, v_scr, pc_scr):
    tm = TM_RWKV
    i = pl.program_id(1)

    @pl.when(i == 0)
    def _():
        shift_scr[0:8, :] = jnp.zeros((8, P_RWKV_W), f32)
        state_scr[...] = jnp.zeros_like(state_scr)

    cur = p_ref[...]
    shift_scr[8:8 + tm, :] = cur
    prev = shift_scr[7:7 + tm, :]
    xs = cur + mu_ref[...] * (prev - cur)
    shift_scr[0:8, :] = shift_scr[tm:tm + 8, :]

    r = xs[:, 0:RWKV_W]
    k = xs[:, RWKV_W:2 * RWKV_W]
    v = xs[:, 2 * RWKV_W:3 * RWKV_W]
    wa = xs[:, 3 * RWKV_W:3 * RWKV_W + LANES]
    gl = xs[:, 3 * RWKV_W + LANES:]

    lane = lax.broadcasted_iota(jnp.int32, (tm, LANES), 1)
    wa = jnp.where(lane < DECAY_LORA, jnp.tanh(wa), wa)
    wa_pre = _dot(wa.astype(bf16), wwa_ref[...])
    z = -(w0_ref[...] + wa_pre[:, :RWKV_W])
    softplus = jnp.maximum(z, 0.0) + jnp.log(1.0 + jnp.exp(-jnp.abs(z)))
    logd = -jnp.exp(-softplus - 0.5)
    a = jax.nn.sigmoid(a0_ref[...] + wa_pre[:, RWKV_W:])
    g = _dot(jax.nn.sigmoid(gl).astype(bf16), g2_ref[...])

    e64 = e64_ref[...]
    kk = k * kk_ref[...]
    kk = kk / jnp.maximum(jnp.sqrt(_dot_split_r(kk * kk, e64)), 1e-12)
    kmod = k * (1.0 + (a - 1.0) * ka_ref[...])
    bonus = _dot((r * kmod * rk_ref[...]).astype(bf16), e64) * v

    cl = _dot_split(tri_ref[...], logd)
    cl3 = cl.reshape(tm // CHUNK, CHUNK, RWKV_W)
    cl_end = jnp.broadcast_to(cl3[:, CHUNK - 1:CHUNK, :], cl3.shape).reshape(tm, RWKV_W)
    at_scr[...] = (-kk * jnp.exp(cl - logd)).astype(bf16)
    rt_scr[...] = (r * jnp.exp(cl)).astype(bf16)
    pinv = jnp.exp(-cl)
    bt_scr[...] = (kk * a * pinv).astype(bf16)
    kt_scr[...] = (kmod * pinv).astype(bf16)
    q_end = jnp.exp(cl_end - cl)
    bp_scr[...] = (kk * a * q_end).astype(bf16)
    kp_scr[...] = (kmod * q_end).astype(bf16)
    v_scr[...] = v.astype(bf16)
    pc_scr[...] = jnp.exp(cl_end)

    c2 = 2 * CHUNK
    lane_c = lax.broadcasted_iota(jnp.int32, (CHUNK, c2), 1)
    row_c = lax.broadcasted_iota(jnp.int32, (CHUNK, c2), 0)
    left = lane_c < CHUNK
    s_idx = jnp.where(left, lane_c, lane_c - CHUNK)
    strict = s_idx < row_c
    incl = s_idx <= row_c
    eye_sbs = (s_idx == row_c).astype(f32)
    level_masks = [((row_c >> (lvl + 1)) == (s_idx >> (lvl + 1))) & (((row_c >> lvl) & 1) == 1)
                   & (((s_idx >> lvl) & 1) == 0) for lvl in range(6)]
    bd_mask = ((lax.broadcasted_iota(jnp.int32, (c2, c2), 0) // CHUNK)
               == (lax.broadcasted_iota(jnp.int32, (c2, c2), 1) // CHUNK))

    def stack2(x):
        zero = jnp.zeros_like(x)
        return jnp.concatenate([jnp.where(left, x, zero), jnp.where(left, zero, x)], axis=0)

    n_chunk = tm // CHUNK
    units = [(c, p) for c in range(n_chunk) for p in range(N_PAIR)]

    def region(c, p):
        return slice(c * CHUNK, (c + 1) * CHUNK), slice(p * LANES, (p + 1) * LANES)

    ar, vb, vst, a_ab, a_ak, a_r, t_inv, w1 = {}, {}, {}, {}, {}, {}, {}, {}
    for u_ in units:
        rows, cols = region(*u_)
        ar[u_] = jnp.concatenate([at_scr[rows, cols], rt_scr[rows, cols]], axis=0)
        yk = jnp.concatenate([stack2(bt_scr[rows, cols]), stack2(kt_scr[rows, cols])], axis=0)
        gm = _dot_nt(ar[u_], yk)
        a_ab[u_] = gm[:CHUNK, :c2]
        a_ak[u_] = jnp.where(strict, gm[:CHUNK, c2:], 0.0).astype(bf16)
        a_r[u_] = jnp.concatenate([jnp.where(incl, gm[CHUNK:, :c2], 0.0),
                                   jnp.where(incl, gm[CHUNK:, c2:], 0.0)], axis=1).astype(bf16)
        t_inv[u_] = eye_sbs + jnp.where(level_masks[0], a_ab[u_], 0.0)
    for lvl in range(1, 6):
        nt = {}
        for u_ in units:
            n_k = jnp.where(level_masks[lvl], a_ab[u_], 0.0).astype(bf16)
            nt[u_] = _dot(n_k, stack2(t_inv[u_].astype(bf16)))
        for u_ in units:
            t_inv[u_] = t_inv[u_] + _dot(t_inv[u_].astype(bf16), stack2(nt[u_].astype(bf16)))
    for u_ in units:
        rows, cols = region(*u_)
        vb[u_] = v_scr[rows, cols]
        vst[u_] = stack2(vb[u_])
        w1[u_] = _dot(a_ak[u_], vst[u_])
        t_inv[u_] = t_inv[u_].astype(bf16)

    ys = [[None] * N_PAIR for _ in range(n_chunk)]
    ht = [state_scr[p] for p in range(N_PAIR)]
    for c in range(n_chunk):
        arh, ub = [None] * N_PAIR, [None] * N_PAIR
        for p in range(N_PAIR):
            arh[p] = _dot_nt(ar[c, p], ht[p].astype(bf16))
        for p in range(N_PAIR):
            ub[p] = _dot(t_inv[c, p], stack2((arh[p][:CHUNK] + w1[c, p]).astype(bf16))).astype(bf16)
        for p in range(N_PAIR):
            rows, cols = region(c, p)
            upd = _dot_tn(jnp.concatenate([ub[p], vb[c, p]], axis=0),
                          jnp.concatenate([bp_scr[rows, cols], kp_scr[rows, cols]], axis=0))
            ht[p] = ht[p] * pc_scr[c * CHUNK:c * CHUNK + 1, cols] + jnp.where(bd_mask, upd, 0.0)
        for p in range(N_PAIR):
            ys[c][p] = arh[p][CHUNK:] + _dot(a_r[c, p], jnp.concatenate([stack2(ub[p]), vst[c, p]], axis=0))
    for p in range(N_PAIR):
        state_scr[p] = ht[p]

    y = jnp.concatenate([jnp.concatenate(row_y, axis=1) for row_y in ys], axis=0)
    inv_n = 1.0 / RWKV_HEAD
    mean = _dot(y.astype(bf16), e64) * inv_n
    yc = y - mean
    var = _dot((yc * yc).astype(bf16), e64) * inv_n
    yn = yc * lax.rsqrt(var + GN_EPS) * gng_ref[...] + gnb_ref[...]
    y_ref[...] = (yn + bonus) * g


def _const_spec(shape):
    return pl.BlockSpec(shape, lambda *_: (0,) * len(shape), pipeline_mode=pl.Buffered(1))


def _params(sem):
    return pltpu.CompilerParams(dimension_semantics=sem, vmem_limit_bytes=VMEM_LIMIT)


def _ffn_in(x, g1, wg, wu, wd, gm, win):
    t = x.shape[0]
    tm = TM_FFN
    row = lambda w: pl.BlockSpec((tm, w), lambda i: (i, 0))
    return pl.pallas_call(
        _ffn_in_kernel,
        grid=(t // tm,),
        in_specs=[row(D_MODEL), _const_spec(g1.shape), _const_spec(wg.shape), _const_spec(wu.shape),
                  _const_spec(wd.shape), _const_spec(gm.shape), _const_spec(win.shape)],
        out_specs=[row(D_MODEL), row(P_MLA_W), row(P_RWKV_W)],
        out_shape=[jax.ShapeDtypeStruct((t, D_MODEL), f32), jax.ShapeDtypeStruct((t, P_MLA_W), f32),
                   jax.ShapeDtypeStruct((t, P_RWKV_W), f32)],
        compiler_params=_params(("parallel",)),
        name="ffn_in",
    )(x, g1, wg, wu, wd, gm, win)


def _out_ffn(x1, o, y, go, gy, wo, g2, wg, wu, wd):
    t = x1.shape[0]
    tm = TM_FFN
    row = lambda w: pl.BlockSpec((tm, w), lambda i: (i, 0))
    return pl.pallas_call(
        _out_ffn_kernel,
        grid=(t // tm,),
        in_specs=[row(D_MODEL), row(MLA_W), row(RWKV_W), _const_spec(go.shape), _const_spec(gy.shape),
                  _const_spec(wo.shape), _const_spec(g2.shape), _const_spec(wg.shape), _const_spec(wu.shape),
                  _const_spec(wd.shape)],
        out_specs=row(D_MODEL),
        out_shape=jax.ShapeDtypeStruct((t, D_MODEL), f32),
        compiler_params=_params(("parallel",)),
        name="out_ffn",
    )(x1, o, y, go, gy, wo, g2, wg, wu, wd)


def _mla_prep(pm, cos_t, sin_t, gq, gkv, wq, wk, wvt, qn, kn, seq):
    t = pm.shape[0]
    tm = TM_PREP
    n_pos = seq // tm
    row = lambda w: pl.BlockSpec((tm, w), lambda i: (i, 0))
    pos = pl.BlockSpec((tm, LANES), lambda i: (i % n_pos, 0))
    return pl.pallas_call(
        _mla_prep_kernel,
        grid=(t // tm,),
        in_specs=[row(P_MLA_W), pos, pos, _const_spec(gq.shape), _const_spec(gkv.shape), _const_spec(wq.shape),
                  _const_spec(wk.shape), _const_spec(wvt.shape), _const_spec(qn.shape), _const_spec(kn.shape)],
        out_specs=[row(QK_PAD_W), row(QK_PAD_W), pl.BlockSpec((MLA_W, tm), lambda i: (0, i))],
        out_shape=[jax.ShapeDtypeStruct((t, QK_PAD_W), bf16), jax.ShapeDtypeStruct((t, QK_PAD_W), bf16),
                   jax.ShapeDtypeStruct((MLA_W, t), bf16)],
        compiler_params=_params(("parallel",)),
        name="mla_prep",
    )(pm, cos_t, sin_t, gq, gkv, wq, wk, wvt, qn, kn)


def _attention(q, k, vt, batch, seq):
    nq = seq // TQ
    return pl.pallas_call(
        _attn_kernel,
        grid=(batch, MLA_HEADS // 2, nq),
        in_specs=[pl.BlockSpec((TQ, 2 * HEAD_PAD), lambda b, hp, i: (b * nq + i, hp)),
                  pl.BlockSpec((seq, 2 * HEAD_PAD), lambda b, hp, i: (b, hp)),
                  pl.BlockSpec((2 * MLA_V, seq), lambda b, hp, i: (hp, b))],
        out_specs=pl.BlockSpec((TQ, 2 * MLA_V), lambda b, hp, i: (b * nq + i, hp)),
        out_shape=jax.ShapeDtypeStruct((batch * seq, MLA_W), f32),
        scratch_shapes=[pltpu.VMEM((2, TQ, TQ), f32)] * 2,
        compiler_params=_params(("parallel", "parallel", "arbitrary")),
        name="attention",
    )(q, k, vt)


def _rwkv(pr, mu, w0, a0, wwa, g2, kk, ka, rk, gng, gnb, e64, tri, batch, seq):
    tm = TM_RWKV
    ns = seq // tm
    vecs = [mu, w0, a0, wwa, g2, kk, ka, rk, gng, gnb, e64, tri]
    return pl.pallas_call(
        _rwkv_kernel,
        grid=(batch, ns),
        in_specs=[pl.BlockSpec((tm, P_RWKV_W), lambda b, i: (b * ns + i, 0))] + [_const_spec(a.shape) for a in vecs],
        out_specs=pl.BlockSpec((tm, RWKV_W), lambda b, i: (b * ns + i, 0)),
        out_shape=jax.ShapeDtypeStruct((batch * seq, RWKV_W), f32),
        scratch_shapes=[pltpu.VMEM((tm + 8, P_RWKV_W), f32),
                        pltpu.VMEM((N_PAIR, LANES, LANES), f32)]
                       + [pltpu.VMEM((tm, RWKV_W), bf16)] * 7
                       + [pltpu.VMEM((tm, RWKV_W), f32)],
        compiler_params=_params(("arbitrary", "arbitrary")),
        name="rwkv",
    )(pr, *vecs)


def _pad_cols(w, n):
    return jnp.pad(w, ((0, 0), (0, n - w.shape[1])))


def _win_ext(w_in):
    d = w_in.shape[0]
    z = lambda n: jnp.zeros((d, n), w_in.dtype)
    pe = w_in[:, Q_LORA + KV_LORA:MLA_IN]
    half = MLA_ROPE // 2
    rw = w_in[:, MLA_IN:]
    o = 0
    r_w = rw[:, o:o + RWKV_W]; o += RWKV_W
    wl_w = rw[:, o:o + DECAY_LORA]; o += DECAY_LORA
    k_w = rw[:, o:o + RWKV_W]; o += RWKV_W
    v_w = rw[:, o:o + RWKV_W]; o += RWKV_W
    al_w = rw[:, o:o + A_LORA]; o += A_LORA
    gl_w = rw[:, o:o + GATE_LORA]
    return jnp.concatenate([
        w_in[:, :Q_LORA + KV_LORA],
        z(MLA_NOPE), pe, z(LANES - MLA_QK),
        z(MLA_NOPE), pe[:, half:], pe[:, :half], z(LANES - MLA_QK),
        r_w, k_w, v_w, wl_w, al_w, gl_w, z(GATE_PAD - GATE_LORA)], axis=1)


def _mu_ext(mu):
    o = 0
    r_m = mu[o:o + RWKV_W]; o += RWKV_W
    wl_m = mu[o:o + DECAY_LORA]; o += DECAY_LORA
    k_m = mu[o:o + RWKV_W]; o += RWKV_W
    v_m = mu[o:o + RWKV_W]; o += RWKV_W
    al_m = mu[o:o + A_LORA]; o += A_LORA
    gl_m = mu[o:o + GATE_LORA]
    return jnp.concatenate([r_m, k_m, v_m, wl_m, al_m, gl_m, jnp.zeros((GATE_PAD - GATE_LORA,), mu.dtype)])[None, :]


def _head_pad_cols(w, width):
    kdim = w.shape[0]
    w = w.reshape(kdim, MLA_HEADS, width)
    return jnp.pad(w, ((0, 0), (0, 0), (0, HEAD_PAD - width))).reshape(kdim, QK_PAD_W)


def _swap_rope_cols(w):
    kdim = w.shape[0]
    half = MLA_ROPE // 2
    w = w.reshape(kdim, MLA_HEADS, MLA_QK)
    sw = jnp.concatenate([jnp.zeros((kdim, MLA_HEADS, MLA_NOPE), w.dtype), w[..., MLA_NOPE + half:],
                          w[..., MLA_NOPE:MLA_NOPE + half],
                          jnp.zeros((kdim, MLA_HEADS, HEAD_PAD - MLA_QK), w.dtype)], axis=-1)
    return sw.reshape(kdim, QK_PAD_W)


def _norm_pair(g):
    half = MLA_ROPE // 2
    zpad = jnp.zeros((HEAD_PAD - MLA_QK,), g.dtype)
    plain = jnp.concatenate([g, zpad])
    sw = jnp.concatenate([jnp.zeros((MLA_NOPE,), g.dtype), g[MLA_NOPE + half:], g[MLA_NOPE:MLA_NOPE + half], zpad])
    return jnp.stack([plain, sw])


def _rope_tiles(seq):
    pos = jnp.arange(seq, dtype=f32)
    inv_freq = ROPE_BASE ** (-jnp.arange(0, MLA_ROPE, 2, dtype=f32) / MLA_ROPE)
    ang = pos[:, None] * inv_freq[None, :]
    cos, sin = jnp.cos(ang), jnp.sin(ang)
    zpad = jnp.zeros((seq, HEAD_PAD - MLA_QK), f32)
    cos_t = jnp.concatenate([jnp.ones((seq, MLA_NOPE), f32), cos, cos, zpad], axis=1)
    sin_t = jnp.concatenate([jnp.zeros((seq, MLA_NOPE), f32), -sin, sin, zpad], axis=1)
    return cos_t, sin_t


def _block_diag_const(n, blk, lower):
    r = jnp.arange(n)[:, None]
    c = jnp.arange(n)[None, :]
    m = (r // blk) == (c // blk)
    if lower:
        m = m & (c <= r)
    return m.astype(bf16)


@jax.jit
def kernel(x, ffn1_norm, ffn1_w_gate, ffn1_w_up, ffn1_w_down, mix_norm, w_in, q_lat_norm, w_q_up, kv_lat_norm, w_kv_up, q_norm, k_norm, shift_mu, w0, w2, a0, a2, g2, k_k, k_a, r_k, gn_gain, gn_bias, out_norm_mla, out_norm_rwkv, w_out, ffn2_norm, ffn2_w_gate, ffn2_w_up, ffn2_w_down):
    batch, seq, d = x.shape
    xt = x.reshape(batch * seq, d)
    cos_t, sin_t = _rope_tiles(seq)
    e64 = _block_diag_const(RWKV_W, RWKV_HEAD, lower=False)
    tri = _block_diag_const(TM_RWKV, CHUNK, lower=True)
    row = lambda a: a[None, :]
    for l in range(DEPTH):
        xt, pm, pr = _ffn_in(xt, row(ffn1_norm[l]), ffn1_w_gate[l].astype(bf16), ffn1_w_up[l].astype(bf16),
                             ffn1_w_down[l].astype(bf16), row(mix_norm[l]), _win_ext(w_in[l]).astype(bf16))
        wq = jnp.concatenate([_head_pad_cols(w_q_up[l], MLA_QK), _swap_rope_cols(w_q_up[l])], axis=1).astype(bf16)
        wkv3 = w_kv_up[l].reshape(KV_LORA, MLA_HEADS, MLA_NOPE + MLA_V)
        wk = _head_pad_cols(wkv3[..., :MLA_NOPE].reshape(KV_LORA, -1), MLA_NOPE).astype(bf16)
        wvt = wkv3[..., MLA_NOPE:].reshape(KV_LORA, MLA_W).T.astype(bf16)
        q, k, vt = _mla_prep(pm, cos_t, sin_t, row(q_lat_norm[l]), row(kv_lat_norm[l]), wq, wk, wvt,
                             _norm_pair(q_norm[l]), _norm_pair(k_norm[l]), seq)
        o = _attention(q, k, vt, batch, seq)
        zl = jnp.zeros((DECAY_LORA, RWKV_W), f32)
        wwa = jnp.concatenate([jnp.concatenate([w2[l], zl], axis=1), jnp.concatenate([zl, a2[l]], axis=1)], axis=0)
        g2p = jnp.pad(g2[l], ((0, GATE_PAD - GATE_LORA), (0, 0)))
        y = _rwkv(pr, _mu_ext(shift_mu[l]), row(w0[l]), row(a0[l]), wwa.astype(bf16), g2p.astype(bf16),
                  row(k_k[l]), row(k_a[l]), row(r_k[l].reshape(-1)), row(gn_gain[l]), row(gn_bias[l]),
                  e64, tri, batch, seq)
        xt = _out_ffn(xt, o, y, row(out_norm_mla[l]), row(out_norm_rwkv[l]), w_out[l].astype(bf16),
                      row(ffn2_norm[l]), ffn2_w_gate[l].astype(bf16), ffn2_w_up[l].astype(bf16),
                      ffn2_w_down[l].astype(bf16))
    return xt.reshape(batch, seq, d)
```

```python
import functools

import jax
import jax.numpy as jnp
from jax import lax
from jax.experimental import pallas as pl
from jax.experimental.pallas import tpu as pltpu

f32 = jnp.float32
bf16 = jnp.bfloat16

D_MODEL = 1024
DEPTH = 4
CHUNK = 64
D_FF = 2816
EPS = 1e-6
MLA_HEADS = 8
MLA_NOPE = 64
MLA_ROPE = 32
MLA_QK = MLA_NOPE + MLA_ROPE
MLA_V = 64
Q_LORA = 256
KV_LORA = 128
ROPE_BASE = 10000.0
RWKV_HEADS = 8
RWKV_HEAD = 64
RWKV_W = RWKV_HEADS * RWKV_HEAD
DECAY_LORA = 64
A_LORA = 64
GATE_LORA = 160
GN_EPS = 64e-5
MLA_W = MLA_HEADS * MLA_V
MLA_IN = Q_LORA + KV_LORA + MLA_ROPE

LANES = 128
HEAD_PAD = LANES
QK_PAD_W = MLA_HEADS * HEAD_PAD
P_MLA_W = Q_LORA + KV_LORA + 2 * LANES
GATE_PAD = 2 * LANES
P_RWKV_W = 3 * RWKV_W + LANES + GATE_PAD
VMEM_LIMIT = 56 * 1024 * 1024

TM_FFN = 256
TM_PREP = 512
TQ = 512
ONES_ROWS = 16
LOG2_E = 1.4426950408889634
TM_RWKV = 256
N_PAIR = RWKV_HEADS // 2


def _rms(x, g):
    return x * lax.rsqrt(jnp.mean(x * x, axis=-1, keepdims=True) + EPS) * g


def _dot(a, b):
    return jnp.dot(a, b, preferred_element_type=f32)


def _dot_nt(a, b):
    return lax.dot_general(a, b, (((1,), (1,)), ((), ())), preferred_element_type=f32)


def _dot_tn(a, b):
    return lax.dot_general(a, b, (((0,), (0,)), ((), ())), preferred_element_type=f32)


def _dot_split(c, x):
    hi = x.astype(bf16)
    lo = (x - hi.astype(f32)).astype(bf16)
    return _dot(c, hi) + _dot(c, lo)


def _dot_split_r(x, c):
    hi = x.astype(bf16)
    lo = (x - hi.astype(f32)).astype(bf16)
    return _dot(hi, c) + _dot(lo, c)


def _swiglu_half(x, g, wg_ref, wu_ref, wd_ref):
    h = _rms(x, g).astype(bf16)
    gate = _dot(h, wg_ref[...])
    up = _dot(h, wu_ref[...])
    act = (jax.nn.silu(gate) * up).astype(bf16)
    return x + 0.5 * _dot(act, wd_ref[...])


def _ffn_in_kernel(x_ref, g1_ref, wg_ref, wu_ref, wd_ref, gm_ref, win_ref, x1_ref, pm_ref, pr_ref):
    x1 = _swiglu_half(x_ref[...], g1_ref[...], wg_ref, wu_ref, wd_ref)
    x1_ref[...] = x1
    p = _dot(_rms(x1, gm_ref[...]).astype(bf16), win_ref[...])
    pm_ref[...] = p[:, :P_MLA_W]
    pr_ref[...] = p[:, P_MLA_W:]


def _out_ffn_kernel(x_ref, o_ref, y_ref, go_ref, gy_ref, wo_ref, g2_ref, wg_ref, wu_ref, wd_ref, x3_ref):
    ho = _rms(o_ref[...], go_ref[...]).astype(bf16)
    hy = _rms(y_ref[...], gy_ref[...]).astype(bf16)
    x2 = x_ref[...] + _dot(ho, wo_ref[:MLA_W, :]) + _dot(hy, wo_ref[MLA_W:, :])
    x3_ref[...] = _swiglu_half(x2, g2_ref[...], wg_ref, wu_ref, wd_ref)


def _mla_prep_kernel(pm_ref, cos_ref, sin_ref, gq_ref, gkv_ref, wq_ref, wk_ref, wvt_ref, qn_ref, kn_ref,
                     q_ref, k_ref, vt_ref):
    pm = pm_ref[...]
    q_lat = pm[:, :Q_LORA]
    kv_lat = pm[:, Q_LORA:Q_LORA + KV_LORA]
    kpe = pm[:, Q_LORA + KV_LORA:Q_LORA + KV_LORA + LANES]
    kpe_sw = pm[:, Q_LORA + KV_LORA + LANES:]
    cos_t = cos_ref[...]
    sin_t = sin_ref[...]
    q2 = _dot(_rms(q_lat, gq_ref[...]).astype(bf16), wq_ref[...])
    hkv = _rms(kv_lat, gkv_ref[...]).astype(bf16)
    kv = _dot(hkv, wk_ref[...])
    vt_ref[...] = _dot_nt(wvt_ref[...], hkv).astype(bf16)
    qc = qn_ref[0:1, :] * cos_t
    qs = qn_ref[1:2, :] * sin_t
    kc = kn_ref[0:1, :] * cos_t
    ks = kn_ref[1:2, :] * sin_t
    scale = MLA_QK ** -0.5 * LOG2_E
    for h in range(MLA_HEADS):
        sl = slice(h * HEAD_PAD, (h + 1) * HEAD_PAD)
        qh = q2[:, sl]
        qsw = q2[:, QK_PAD_W + h * HEAD_PAD:QK_PAD_W + (h + 1) * HEAD_PAD]
        rs = lax.rsqrt(jnp.sum(qh * qh, axis=-1, keepdims=True) * (1.0 / MLA_QK) + EPS)
        q_ref[:, sl] = ((qh * qc + qsw * qs) * (rs * scale)).astype(bf16)
        kh = kv[:, sl] + kpe
        rk = lax.rsqrt(jnp.sum(kh * kh, axis=-1, keepdims=True) * (1.0 / MLA_QK) + EPS)
        k_ref[:, sl] = ((kh * kc + kpe_sw * ks) * rk).astype(bf16)


def _attn_kernel(q_ref, k_ref, vt_ref, o_ref, s0_scr, s1_scr):
    n_tiles = q_ref.shape[0] // TQ
    chunk_gap = (lax.broadcasted_iota(jnp.int32, (TQ, TQ), 0) // CHUNK
                 - lax.broadcasted_iota(jnp.int32, (TQ, TQ), 1) // CHUNK)
    heads = [slice(h * HEAD_PAD, (h + 1) * HEAD_PAD) for h in range(2)]
    ones_rows = jnp.ones((ONES_ROWS, TQ), bf16)
    bufs = (s0_scr, s1_scr)
    tile = lambda n: slice(n * TQ, (n + 1) * TQ)

    def scores(step, s_scr):
        i, j = step
        for h in range(2):
            s_scr[h] = _dot_nt(k_ref[tile(j), heads[h]], q_ref[tile(i), heads[h]])

    def consume(step, s_scr, stats):
        i, j = step
        new = []
        for h in range(2):
            m, acc = stats[h]
            st = s_scr[h]
            if j == i:
                st = jnp.where(chunk_gap <= 0, st, -jnp.inf)
            m_new = jnp.maximum(m, jnp.max(st, axis=0, keepdims=True))
            alpha = jnp.exp2(m - m_new)
            pr = jnp.exp2(st - m_new).astype(bf16)
            vt_aug = jnp.concatenate([vt_ref[h * MLA_V:(h + 1) * MLA_V, tile(j)], ones_rows], axis=0)
            new.append((m_new, alpha * acc + _dot(vt_aug, pr)))
        return tuple(new)

    steps = [(i, j) for i in range(n_tiles) for j in range(i + 1)]
    scores(steps[0], bufs[0])
    stats = None
    for n, step in enumerate(steps):
        i, j = step
        if n + 1 < len(steps):
            scores(steps[n + 1], bufs[(n + 1) % 2])
        if j == 0:
            stats = tuple((jnp.full((1, TQ), -jnp.inf, f32), jnp.zeros((MLA_V + ONES_ROWS, TQ), f32))
                          for _ in range(2))
        stats = consume(step, bufs[n % 2], stats)
        if j == i:
            (_, acc0), (_, acc1) = stats
            o_ref[tile(i), :] = jnp.concatenate([acc0[:MLA_V] / acc0[MLA_V:MLA_V + 1],
                                                 acc1[:MLA_V] / acc1[MLA_V:MLA_V + 1]], axis=0).T


def _rwkv_kernel(p_ref, mu_ref, w0_ref, a0_ref, wwa_ref, g2_ref, kk_ref, ka_ref, rk_ref, gng_ref, gnb_ref,
                 e64_ref, tri_ref, y_ref,
                 shift_scr, state_scr, at_scr, rt_scr, bt_scr, kt_scr, bp_scr, kp_scr, v_scr, pc_scr):
    tm = TM_RWKV
    i = pl.program_id(1)

    @pl.when(i == 0)
    def _():
        shift_scr[0:8, :] = jnp.zeros((8, P_RWKV_W), f32)
        state_scr[...] = jnp.zeros_like(state_scr)

    cur = p_ref[...]
    shift_scr[8:8 + tm, :] = cur
    prev = shift_scr[7:7 + tm, :]
    xs = cur + mu_ref[...] * (prev - cur)
    shift_scr[0:8, :] = shift_scr[tm:tm + 8, :]

    r = xs[:, 0:RWKV_W]
    k = xs[:, RWKV_W:2 * RWKV_W]
    v = xs[:, 2 * RWKV_W:3 * RWKV_W]
    wa = xs[:, 3 * RWKV_W:3 * RWKV_W + LANES]
    gl = xs[:, 3 * RWKV_W + LANES:]

    lane = lax.broadcasted_iota(jnp.int32, (tm, LANES), 1)
    wa = jnp.where(lane < DECAY_LORA, jnp.tanh(wa), wa)
    wa_pre = _dot(wa.astype(bf16), wwa_ref[...])
    z = -(w0_ref[...] + wa_pre[:, :RWKV_W])
    softplus = jnp.maximum(z, 0.0) + jnp.log(1.0 + jnp.exp(-jnp.abs(z)))
    logd = -jnp.exp(-softplus - 0.5)
    a = jax.nn.sigmoid(a0_ref[...] + wa_pre[:, RWKV_W:])
    g = _dot(jax.nn.sigmoid(gl).astype(bf16), g2_ref[...])

    e64 = e64_ref[...]
    kk = k * kk_ref[...]
    kk = kk / jnp.maximum(jnp.sqrt(_dot_split_r(kk * kk, e64)), 1e-12)
    kmod = k * (1.0 + (a - 1.0) * ka_ref[...])
    bonus = _dot((r * kmod * rk_ref[...]).astype(bf16), e64) * v

    cl = _dot_split(tri_ref[...], logd)
    cl3 = cl.reshape(tm // CHUNK, CHUNK, RWKV_W)
    cl_end = jnp.broadcast_to(cl3[:, CHUNK - 1:CHUNK, :], cl3.shape).reshape(tm, RWKV_W)
    at_scr[...] = (-kk * jnp.exp(cl - logd)).astype(bf16)
    rt_scr[...] = (r * jnp.exp(cl)).astype(bf16)
    pinv = jnp.exp(-cl)
    bt_scr[...] = (kk * a * pinv).astype(bf16)
    kt_scr[...] = (kmod * pinv).astype(bf16)
    q_end = jnp.exp(cl_end - cl)
    bp_scr[...] = (kk * a * q_end).astype(bf16)
    kp_scr[...] = (kmod * q_end).astype(bf16)
    v_scr[...] = v.astype(bf16)
    pc_scr[...] = jnp.exp(cl_end)

    c2 = 2 * CHUNK
    lane_c = lax.broadcasted_iota(jnp.int32, (CHUNK, c2), 1)
    row_c = lax.broadcasted_iota(jnp.int32, (CHUNK, c2), 0)
    left = lane_c < CHUNK
    s_idx = jnp.where(left, lane_c, lane_c - CHUNK)
    strict = s_idx < row_c
    incl = s_idx <= row_c
    eye_sbs = (s_idx == row_c).astype(f32)
    level_masks = [((row_c >> (lvl + 1)) == (s_idx >> (lvl + 1))) & (((row_c >> lvl) & 1) == 1)
                   & (((s_idx >> lvl) & 1) == 0) for lvl in range(6)]
    bd_mask = ((lax.broadcasted_iota(jnp.int32, (c2, c2), 0) // CHUNK)
               == (lax.broadcasted_iota(jnp.int32, (c2, c2), 1) // CHUNK))

    def stack2(x):
        zero = jnp.zeros_like(x)
        return jnp.concatenate([jnp.where(left, x, zero), jnp.where(left, zero, x)], axis=0)

    n_chunk = tm // CHUNK
    units = [(c, p) for c in range(n_chunk) for p in range(N_PAIR)]

    def region(c, p):
        return slice(c * CHUNK, (c + 1) * CHUNK), slice(p * LANES, (p + 1) * LANES)

    ar, vb, vst, a_ab, a_ak, a_r, t_inv, w1 = {}, {}, {}, {}, {}, {}, {}, {}
    for u_ in units:
        rows, cols = region(*u_)
        ar[u_] = jnp.concatenate([at_scr[rows, cols], rt_scr[rows, cols]], axis=0)
        yk = jnp.concatenate([stack2(bt_scr[rows, cols]), stack2(kt_scr[rows, cols])], axis=0)
        gm = _dot_nt(ar[u_], yk)
        a_ab[u_] = gm[:CHUNK, :c2]
        a_ak[u_] = jnp.where(strict, gm[:CHUNK, c2:], 0.0).astype(bf16)
        a_r[u_] = jnp.concatenate([jnp.where(incl, gm[CHUNK:, :c2], 0.0),
                                   jnp.where(incl, gm[CHUNK:, c2:], 0.0)], axis=1).astype(bf16)
        t_inv[u_] = eye_sbs + jnp.where(level_masks[0], a_ab[u_], 0.0)
    for lvl in range(1, 6):
        nt = {}
        for u_ in units:
            n_k = jnp.where(level_masks[lvl], a_ab[u_], 0.0).astype(bf16)
            nt[u_] = _dot(n_k, stack2(t_inv[u_].astype(bf16)))
        for u_ in units:
            t_inv[u_] = t_inv[u_] + _dot(t_inv[u_].astype(bf16), stack2(nt[u_].astype(bf16)))
    for u_ in units:
        rows, cols = region(*u_)
        vb[u_] = v_scr[rows, cols]
        vst[u_] = stack2(vb[u_])
        w1[u_] = _dot(a_ak[u_], vst[u_])
    a2, u1 = {}, {}
    for u_ in units:
        au = _dot(t_inv[u_].astype(bf16),
                  jnp.concatenate([stack2(ar[u_][:CHUNK]), stack2(w1[u_].astype(bf16))], axis=1))
        a2[u_] = au[:, :c2].astype(bf16)
        u1[u_] = au[:, c2:].astype(bf16)
    m_c, n_c, rt2, y1 = {}, {}, {}, {}
    for u_ in units:
        rows, cols = region(*u_)
        lhs = jnp.concatenate([jnp.concatenate([a2[u_], u1[u_]], axis=1),
                               jnp.concatenate([jnp.zeros_like(vb[u_]), vb[u_]], axis=1)], axis=0)
        mn = _dot_tn(lhs, jnp.concatenate([bp_scr[rows, cols], kp_scr[rows, cols]], axis=0))
        m_c[u_] = jnp.where(bd_mask, mn[:c2], 0.0).astype(bf16)
        n_c[u_] = jnp.where(bd_mask, mn[c2:], 0.0)
    for u_ in units:
        rt2[u_] = (ar[u_][CHUNK:].astype(f32) + _dot(a_r[u_][:, :c2], stack2(a2[u_]))).astype(bf16)
    for u_ in units:
        y1[u_] = _dot(a_r[u_], jnp.concatenate([stack2(u1[u_]), vst[u_]], axis=0))

    ys = [[None] * N_PAIR for _ in range(n_chunk)]
    ht = [state_scr[p] for p in range(N_PAIR)]
    for c in range(n_chunk):
        hb = [ht[p].astype(bf16) for p in range(N_PAIR)]
        for p in range(N_PAIR):
            cols = region(c, p)[1]
            ht[p] = ht[p] * pc_scr[c * CHUNK:c * CHUNK + 1, cols] + _dot(hb[p], m_c[c, p]) + n_c[c, p]
        for p in range(N_PAIR):
            ys[c][p] = _dot_nt(rt2[c, p], hb[p]) + y1[c, p]
    for p in range(N_PAIR):
        state_scr[p] = ht[p]

    y = jnp.concatenate([jnp.concatenate(row_y, axis=1) for row_y in ys], axis=0)
    inv_n = 1.0 / RWKV_HEAD
    mean = _dot(y.astype(bf16), e64) * inv_n
    yc = y - mean
    var = _dot((yc * yc).astype(bf16), e64) * inv_n
    yn = yc * lax.rsqrt(var + GN_EPS) * gng_ref[...] + gnb_ref[...]
    y_ref[...] = (yn + bonus) * g


def _const_spec(shape):
    return pl.BlockSpec(shape, lambda *_: (0,) * len(shape), pipeline_mode=pl.Buffered(1))


def _params(sem):
    return pltpu.CompilerParams(dimension_semantics=sem, vmem_limit_bytes=VMEM_LIMIT)


def _ffn_in(x, g1, wg, wu, wd, gm, win):
    t = x.shape[0]
    tm = TM_FFN
    row = lambda w: pl.BlockSpec((tm, w), lambda i: (i, 0))
    return pl.pallas_call(
        _ffn_in_kernel,
        grid=(t // tm,),
        in_specs=[row(D_MODEL), _const_spec(g1.shape), _const_spec(wg.shape), _const_spec(wu.shape),
                  _const_spec(wd.shape), _const_spec(gm.shape), _const_spec(win.shape)],
        out_specs=[row(D_MODEL), row(P_MLA_W), row(P_RWKV_W)],
        out_shape=[jax.ShapeDtypeStruct((t, D_MODEL), f32), jax.ShapeDtypeStruct((t, P_MLA_W), f32),
                   jax.ShapeDtypeStruct((t, P_RWKV_W), f32)],
        compiler_params=_params(("parallel",)),
        name="ffn_in",
    )(x, g1, wg, wu, wd, gm, win)


def _out_ffn(x1, o, y, go, gy, wo, g2, wg, wu, wd):
    t = x1.shape[0]
    tm = TM_FFN
    row = lambda w: pl.BlockSpec((tm, w), lambda i: (i, 0))
    return pl.pallas_call(
        _out_ffn_kernel,
        grid=(t // tm,),
        in_specs=[row(D_MODEL), row(MLA_W), row(RWKV_W), _const_spec(go.shape), _const_spec(gy.shape),
                  _const_spec(wo.shape), _const_spec(g2.shape), _const_spec(wg.shape), _const_spec(wu.shape),
                  _const_spec(wd.shape)],
        out_specs=row(D_MODEL),
        out_shape=jax.ShapeDtypeStruct((t, D_MODEL), f32),
        compiler_params=_params(("parallel",)),
        name="out_ffn",
    )(x1, o, y, go, gy, wo, g2, wg, wu, wd)


def _mla_prep(pm, cos_t, sin_t, gq, gkv, wq, wk, wvt, qn, kn, seq):
    t = pm.shape[0]
    tm = TM_PREP
    n_pos = seq // tm
    row = lambda w: pl.BlockSpec((tm, w), lambda i: (i, 0))
    pos = pl.BlockSpec((tm, LANES), lambda i: (i % n_pos, 0))
    return pl.pallas_call(
        _mla_prep_kernel,
        grid=(t // tm,),
        in_specs=[row(P_MLA_W), pos, pos, _const_spec(gq.shape), _const_spec(gkv.shape), _const_spec(wq.shape),
                  _const_spec(wk.shape), _const_spec(wvt.shape), _const_spec(qn.shape), _const_spec(kn.shape)],
        out_specs=[row(QK_PAD_W), row(QK_PAD_W), pl.BlockSpec((MLA_W, tm), lambda i: (0, i))],
        out_shape=[jax.ShapeDtypeStruct((t, QK_PAD_W), bf16), jax.ShapeDtypeStruct((t, QK_PAD_W), bf16),
                   jax.ShapeDtypeStruct((MLA_W, t), bf16)],
        compiler_params=_params(("parallel",)),
        name="mla_prep",
    )(pm, cos_t, sin_t, gq, gkv, wq, wk, wvt, qn, kn)


def _attention(q, k, vt, batch, seq):
    return pl.pallas_call(
        _attn_kernel,
        grid=(batch, MLA_HEADS // 2),
        in_specs=[pl.BlockSpec((seq, 2 * HEAD_PAD), lambda b, hp: (b, hp)),
                  pl.BlockSpec((seq, 2 * HEAD_PAD), lambda b, hp: (b, hp)),
                  pl.BlockSpec((2 * MLA_V, seq), lambda b, hp: (hp, b))],
        out_specs=pl.BlockSpec((seq, 2 * MLA_V), lambda b, hp: (b, hp)),
        out_shape=jax.ShapeDtypeStruct((batch * seq, MLA_W), f32),
        scratch_shapes=[pltpu.VMEM((2, TQ, TQ), f32)] * 2,
        compiler_params=_params(("parallel", "parallel")),
        name="attention",
    )(q, k, vt)


def _rwkv(pr, mu, w0, a0, wwa, g2, kk, ka, rk, gng, gnb, e64, tri, batch, seq):
    tm = TM_RWKV
    ns = seq // tm
    vecs = [mu, w0, a0, wwa, g2, kk, ka, rk, gng, gnb, e64, tri]
    return pl.pallas_call(
        _rwkv_kernel,
        grid=(batch, ns),
        in_specs=[pl.BlockSpec((tm, P_RWKV_W), lambda b, i: (b * ns + i, 0))] + [_const_spec(a.shape) for a in vecs],
        out_specs=pl.BlockSpec((tm, RWKV_W), lambda b, i: (b * ns + i, 0)),
        out_shape=jax.ShapeDtypeStruct((batch * seq, RWKV_W), f32),
        scratch_shapes=[pltpu.VMEM((tm + 8, P_RWKV_W), f32),
                        pltpu.VMEM((N_PAIR, LANES, LANES), f32)]
                       + [pltpu.VMEM((tm, RWKV_W), bf16)] * 7
                       + [pltpu.VMEM((tm, RWKV_W), f32)],
        compiler_params=_params(("arbitrary", "arbitrary")),
        name="rwkv",
    )(pr, *vecs)


def _pad_cols(w, n):
    return jnp.pad(w, ((0, 0), (0, n - w.shape[1])))


def _win_ext(w_in):
    d = w_in.shape[0]
    z = lambda n: jnp.zeros((d, n), w_in.dtype)
    pe = w_in[:, Q_LORA + KV_LORA:MLA_IN]
    half = MLA_ROPE // 2
    rw = w_in[:, MLA_IN:]
    o = 0
    r_w = rw[:, o:o + RWKV_W]; o += RWKV_W
    wl_w = rw[:, o:o + DECAY_LORA]; o += DECAY_LORA
    k_w = rw[:, o:o + RWKV_W]; o += RWKV_W
    v_w = rw[:, o:o + RWKV_W]; o += RWKV_W
    al_w = rw[:, o:o + A_LORA]; o += A_LORA
    gl_w = rw[:, o:o + GATE_LORA]
    return jnp.concatenate([
        w_in[:, :Q_LORA + KV_LORA],
        z(MLA_NOPE), pe, z(LANES - MLA_QK),
        z(MLA_NOPE), pe[:, half:], pe[:, :half], z(LANES - MLA_QK),
        r_w, k_w, v_w, wl_w, al_w, gl_w, z(GATE_PAD - GATE_LORA)], axis=1)


def _mu_ext(mu):
    o = 0
    r_m = mu[o:o + RWKV_W]; o += RWKV_W
    wl_m = mu[o:o + DECAY_LORA]; o += DECAY_LORA
    k_m = mu[o:o + RWKV_W]; o += RWKV_W
    v_m = mu[o:o + RWKV_W]; o += RWKV_W
    al_m = mu[o:o + A_LORA]; o += A_LORA
    gl_m = mu[o:o + GATE_LORA]
    return jnp.concatenate([r_m, k_m, v_m, wl_m, al_m, gl_m, jnp.zeros((GATE_PAD - GATE_LORA,), mu.dtype)])[None, :]


def _head_pad_cols(w, width):
    kdim = w.shape[0]
    w = w.reshape(kdim, MLA_HEADS, width)
    return jnp.pad(w, ((0, 0), (0, 0), (0, HEAD_PAD - width))).reshape(kdim, QK_PAD_W)


def _swap_rope_cols(w):
    kdim = w.shape[0]
    half = MLA_ROPE // 2
    w = w.reshape(kdim, MLA_HEADS, MLA_QK)
    sw = jnp.concatenate([jnp.zeros((kdim, MLA_HEADS, MLA_NOPE), w.dtype), w[..., MLA_NOPE + half:],
                          w[..., MLA_NOPE:MLA_NOPE + half],
                          jnp.zeros((kdim, MLA_HEADS, HEAD_PAD - MLA_QK), w.dtype)], axis=-1)
    return sw.reshape(kdim, QK_PAD_W)


def _norm_pair(g):
    half = MLA_ROPE // 2
    zpad = jnp.zeros((HEAD_PAD - MLA_QK,), g.dtype)
    plain = jnp.concatenate([g, zpad])
    sw = jnp.concatenate([jnp.zeros((MLA_NOPE,), g.dtype), g[MLA_NOPE + half:], g[MLA_NOPE:MLA_NOPE + half], zpad])
    return jnp.stack([plain, sw])


def _rope_tiles(seq):
    pos = jnp.arange(seq, dtype=f32)
    inv_freq = ROPE_BASE ** (-jnp.arange(0, MLA_ROPE, 2, dtype=f32) / MLA_ROPE)
    ang = pos[:, None] * inv_freq[None, :]
    cos, sin = jnp.cos(ang), jnp.sin(ang)
    zpad = jnp.zeros((seq, HEAD_PAD - MLA_QK), f32)
    cos_t = jnp.concatenate([jnp.ones((seq, MLA_NOPE), f32), cos, cos, zpad], axis=1)
    sin_t = jnp.concatenate([jnp.zeros((seq, MLA_NOPE), f32), -sin, sin, zpad], axis=1)
    return cos_t, sin_t


def _block_diag_const(n, blk, lower):
    r = jnp.arange(n)[:, None]
    c = jnp.arange(n)[None, :]
    m = (r // blk) == (c // blk)
    if lower:
        m = m & (c <= r)
    return m.astype(bf16)


@jax.jit
def kernel(x, ffn1_norm, ffn1_w_gate, ffn1_w_up, ffn1_w_down, mix_norm, w_in, q_lat_norm, w_q_up, kv_lat_norm, w_kv_up, q_norm, k_norm, shift_mu, w0, w2, a0, a2, g2, k_k, k_a, r_k, gn_gain, gn_bias, out_norm_mla, out_norm_rwkv, w_out, ffn2_norm, ffn2_w_gate, ffn2_w_up, ffn2_w_down):
    batch, seq, d = x.shape
    xt = x.reshape(batch * seq, d)
    cos_t, sin_t = _rope_tiles(seq)
    e64 = _block_diag_const(RWKV_W, RWKV_HEAD, lower=False)
    tri = _block_diag_const(TM_RWKV, CHUNK, lower=True)
    row = lambda a: a[None, :]
    for l in range(DEPTH):
        xt, pm, pr = _ffn_in(xt, row(ffn1_norm[l]), ffn1_w_gate[l].astype(bf16), ffn1_w_up[l].astype(bf16),
                             ffn1_w_down[l].astype(bf16), row(mix_norm[l]), _win_ext(w_in[l]).astype(bf16))
        wq = jnp.concatenate([_head_pad_cols(w_q_up[l], MLA_QK), _swap_rope_cols(w_q_up[l])], axis=1).astype(bf16)
        wkv3 = w_kv_up[l].reshape(KV_LORA, MLA_HEADS, MLA_NOPE + MLA_V)
        wk = _head_pad_cols(wkv3[..., :MLA_NOPE].reshape(KV_LORA, -1), MLA_NOPE).astype(bf16)
        wvt = wkv3[..., MLA_NOPE:].reshape(KV_LORA, MLA_W).T.astype(bf16)
        q, k, vt = _mla_prep(pm, cos_t, sin_t, row(q_lat_norm[l]), row(kv_lat_norm[l]), wq, wk, wvt,
                             _norm_pair(q_norm[l]), _norm_pair(k_norm[l]), seq)
        o = _attention(q, k, vt, batch, seq)
        zl = jnp.zeros((DECAY_LORA, RWKV_W), f32)
        wwa = jnp.concatenate([jnp.concatenate([w2[l], zl], axis=1), jnp.concatenate([zl, a2[l]], axis=1)], axis=0)
        g2p = jnp.pad(g2[l], ((0, GATE_PAD - GATE_LORA), (0, 0)))
        y = _rwkv(pr, _mu_ext(shift_mu[l]), row(w0[l]), row(a0[l]), wwa.astype(bf16), g2p.astype(bf16),
                  row(k_k[l]), row(k_a[l]), row(r_k[l].reshape(-1)), row(gn_gain[l]), row(gn_bias[l]),
                  e64, tri, batch, seq)
        xt = _out_ffn(xt, o, y, row(out_norm_mla[l]), row(out_norm_rwkv[l]), w_out[l].astype(bf16),
                      row(ffn2_norm[l]), ffn2_w_gate[l].astype(bf16), ffn2_w_up[l].astype(bf16),
                      ffn2_w_down[l].astype(bf16))
    return xt.reshape(batch, seq, d)
```

```python
import jax
import jax.numpy as jnp
from jax import lax
from jax.experimental import pallas as pl
from jax.experimental.pallas import tpu as pltpu

f32 = jnp.float32
bf16 = jnp.bfloat16

D_MODEL = 1024
DEPTH = 4
CHUNK = 64
D_FF = 2816
EPS = 1e-6
MLA_HEADS = 8
MLA_NOPE = 64
MLA_ROPE = 32
MLA_QK = MLA_NOPE + MLA_ROPE
MLA_V = 64
Q_LORA = 256
KV_LORA = 128
ROPE_BASE = 10000.0
RWKV_HEADS = 8
RWKV_HEAD = 64
RWKV_W = RWKV_HEADS * RWKV_HEAD
DECAY_LORA = 64
A_LORA = 64
GATE_LORA = 160
GN_EPS = 64e-5
MLA_W = MLA_HEADS * MLA_V
MLA_IN = Q_LORA + KV_LORA + MLA_ROPE

LANES = 128
HEAD_PAD = LANES
QK_PAD_W = MLA_HEADS * HEAD_PAD
P_MLA_W = Q_LORA + KV_LORA + 2 * LANES
GATE_PAD = 2 * LANES
P_RWKV_W = 3 * RWKV_W + LANES + GATE_PAD
VMEM_LIMIT = 56 * 1024 * 1024

TM_FFN = 256
TM_PREP = 512
TQ = 512
HALF_TQ = TQ // 2
ONES_ROWS = 16
LOG2_E = 1.4426950408889634
TM_RWKV = 256
N_PAIR = RWKV_HEADS // 2


def _rms(x, g):
    return x * lax.rsqrt(jnp.mean(x * x, axis=-1, keepdims=True) + EPS) * g


def _dot(a, b):
    return jnp.dot(a, b, preferred_element_type=f32)


def _dot_nt(a, b):
    return lax.dot_general(a, b, (((1,), (1,)), ((), ())), preferred_element_type=f32)


def _dot_tn(a, b):
    return lax.dot_general(a, b, (((0,), (0,)), ((), ())), preferred_element_type=f32)


def _dot_split(c, x):
    hi = x.astype(bf16)
    lo = (x - hi.astype(f32)).astype(bf16)
    return _dot(c, hi) + _dot(c, lo)


def _swiglu_half(x, g, wg_ref, wu_ref, wd_ref):
    h = _rms(x, g).astype(bf16)
    gate = _dot(h, wg_ref[...])
    up = _dot(h, wu_ref[...])
    act = (jax.nn.silu(gate) * up).astype(bf16)
    return x + 0.5 * _dot(act, wd_ref[...])


def _ffn_in_kernel(x_ref, g1_ref, wg_ref, wu_ref, wd_ref, gm_ref, win_ref, x1_ref, pm_ref, pr_ref):
    x1 = _swiglu_half(x_ref[...], g1_ref[...], wg_ref, wu_ref, wd_ref)
    x1_ref[...] = x1
    p = _dot(_rms(x1, gm_ref[...]).astype(bf16), win_ref[...])
    pm_ref[...] = p[:, :P_MLA_W]
    pr_ref[...] = p[:, P_MLA_W:]


def _out_ffn_kernel(x_ref, o_ref, y_ref, go_ref, gy_ref, wo_ref, g2_ref, wg_ref, wu_ref, wd_ref, x3_ref):
    ho = _rms(o_ref[...], go_ref[...]).astype(bf16)
    hy = _rms(y_ref[...], gy_ref[...]).astype(bf16)
    x2 = x_ref[...] + _dot(ho, wo_ref[:MLA_W, :]) + _dot(hy, wo_ref[MLA_W:, :])
    x3_ref[...] = _swiglu_half(x2, g2_ref[...], wg_ref, wu_ref, wd_ref)


def _mla_prep_kernel(pm_ref, cos_ref, sin_ref, gq_ref, gkv_ref, wq_ref, wk_ref, wvt_ref, qn_ref, kn_ref,
                     q_ref, k_ref, vt_ref):
    pm = pm_ref[...]
    q_lat = pm[:, :Q_LORA]
    kv_lat = pm[:, Q_LORA:Q_LORA + KV_LORA]
    kpe = pm[:, Q_LORA + KV_LORA:Q_LORA + KV_LORA + LANES]
    kpe_sw = pm[:, Q_LORA + KV_LORA + LANES:]
    cos_t = cos_ref[...]
    sin_t = sin_ref[...]
    q2 = _dot(_rms(q_lat, gq_ref[...]).astype(bf16), wq_ref[...])
    hkv = _rms(kv_lat, gkv_ref[...]).astype(bf16)
    kv = _dot(hkv, wk_ref[...])
    vt_ref[...] = _dot_nt(wvt_ref[...], hkv).astype(bf16)
    qc = qn_ref[0:1, :] * cos_t
    qs = qn_ref[1:2, :] * sin_t
    kc = kn_ref[0:1, :] * cos_t
    ks = kn_ref[1:2, :] * sin_t
    scale = MLA_QK ** -0.5 * LOG2_E
    for h in range(MLA_HEADS):
        sl = slice(h * HEAD_PAD, (h + 1) * HEAD_PAD)
        qh = q2[:, sl]
        qsw = q2[:, QK_PAD_W + h * HEAD_PAD:QK_PAD_W + (h + 1) * HEAD_PAD]
        rs = lax.rsqrt(jnp.sum(qh * qh, axis=-1, keepdims=True) * (1.0 / MLA_QK) + EPS)
        q_ref[:, sl] = ((qh * qc + qsw * qs) * (rs * scale)).astype(bf16)
        kh = kv[:, sl] + kpe
        rk = lax.rsqrt(jnp.sum(kh * kh, axis=-1, keepdims=True) * (1.0 / MLA_QK) + EPS)
        k_ref[:, sl] = ((kh * kc + kpe_sw * ks) * rk).astype(bf16)


def _attn_kernel(q_ref, k_ref, vt_ref, o_ref, s0_scr, s1_scr):
    n_tiles = q_ref.shape[0] // TQ
    chunk_gap = (lax.broadcasted_iota(jnp.int32, (HALF_TQ, HALF_TQ), 0) // CHUNK
                 - lax.broadcasted_iota(jnp.int32, (HALF_TQ, HALF_TQ), 1) // CHUNK)
    diag_visible = chunk_gap <= 0
    heads = [slice(h * HEAD_PAD, (h + 1) * HEAD_PAD) for h in range(2)]
    ones_rows = jnp.ones((ONES_ROWS, TQ), bf16)
    bufs = (s0_scr, s1_scr)
    tile = lambda n: slice(n * TQ, (n + 1) * TQ)
    lo_half = lambda n: slice(n * TQ, n * TQ + HALF_TQ)
    hi_half = lambda n: slice(n * TQ + HALF_TQ, (n + 1) * TQ)

    def scores(step, s_scr):
        i, j = step
        for h in range(2):
            if j == i:
                s_scr[h, :HALF_TQ, :] = _dot_nt(k_ref[lo_half(j), heads[h]], q_ref[tile(i), heads[h]])
                s_scr[h, HALF_TQ:, HALF_TQ:] = _dot_nt(k_ref[hi_half(j), heads[h]], q_ref[hi_half(i), heads[h]])
            else:
                s_scr[h] = _dot_nt(k_ref[tile(j), heads[h]], q_ref[tile(i), heads[h]])

    def vt_aug(h, cols):
        return jnp.concatenate([vt_ref[h * MLA_V:(h + 1) * MLA_V, cols], ones_rows[:, :cols.stop - cols.start]], axis=0)

    def consume(step, s_scr, stats):
        i, j = step
        new = []
        for h in range(2):
            m, acc = stats[h]
            if j == i:
                top_lo = jnp.where(diag_visible, s_scr[h, :HALF_TQ, :HALF_TQ], -jnp.inf)
                top_hi = s_scr[h, :HALF_TQ, HALF_TQ:]
                bot_hi = jnp.where(diag_visible, s_scr[h, HALF_TQ:, HALF_TQ:], -jnp.inf)
                m_blk = jnp.concatenate(
                    [jnp.max(top_lo, axis=0, keepdims=True),
                     jnp.maximum(jnp.max(top_hi, axis=0, keepdims=True), jnp.max(bot_hi, axis=0, keepdims=True))],
                    axis=1)
                m_new = jnp.maximum(m, m_blk)
                alpha = jnp.exp2(m - m_new)
                p_top = jnp.exp2(jnp.concatenate([top_lo, top_hi], axis=1) - m_new).astype(bf16)
                p_bot = jnp.exp2(bot_hi - m_new[:, HALF_TQ:]).astype(bf16)
                upd = _dot(vt_aug(h, lo_half(j)), p_top)
                upd_hi = _dot(vt_aug(h, hi_half(j)), p_bot)
                upd = jnp.concatenate([upd[:, :HALF_TQ], upd[:, HALF_TQ:] + upd_hi], axis=1)
            else:
                st = s_scr[h]
                m_new = jnp.maximum(m, jnp.max(st, axis=0, keepdims=True))
                alpha = jnp.exp2(m - m_new)
                upd = _dot(vt_aug(h, tile(j)), jnp.exp2(st - m_new).astype(bf16))
            new.append((m_new, alpha * acc + upd))
        return tuple(new)

    steps = [(i, j) for i in range(n_tiles) for j in range(i + 1)]
    scores(steps[0], bufs[0])
    stats = None
    for n, step in enumerate(steps):
        i, j = step
        if n + 1 < len(steps):
            scores(steps[n + 1], bufs[(n + 1) % 2])
        if j == 0:
            stats = tuple((jnp.full((1, TQ), -jnp.inf, f32), jnp.zeros((MLA_V + ONES_ROWS, TQ), f32))
                          for _ in range(2))
        stats = consume(step, bufs[n % 2], stats)
        if j == i:
            (_, acc0), (_, acc1) = stats
            o_ref[tile(i), :] = jnp.concatenate([acc0[:MLA_V] / acc0[MLA_V:MLA_V + 1],
                                                 acc1[:MLA_V] / acc1[MLA_V:MLA_V + 1]], axis=0).T


def _rwkv_kernel(p_ref, mu_ref, w0_ref, a0_ref, wwa_ref, g2_ref, kk_ref, ka_ref, rk_ref, gng_ref, gnb_ref,
                 e64_ref, tri_ref, y_ref,
                 shift_scr, state_scr, at_scr, rt_scr, bt_scr, kt_scr, bp_scr, kp_scr, v_scr, pc_scr):
    tm = TM_RWKV
    i = pl.program_id(1)

    @pl.when(i == 0)
    def _():
        shift_scr[0:8, :] = jnp.zeros((8, P_RWKV_W), f32)
        state_scr[...] = jnp.zeros_like(state_scr)

    cur = p_ref[...]
    shift_scr[8:8 + tm, :] = cur
    prev = shift_scr[7:7 + tm, :]
    xs = cur + mu_ref[...] * (prev - cur)
    shift_scr[0:8, :] = shift_scr[tm:tm + 8, :]

    r = xs[:, 0:RWKV_W]
    k = xs[:, RWKV_W:2 * RWKV_W]
    v = xs[:, 2 * RWKV_W:3 * RWKV_W]
    wa = xs[:, 3 * RWKV_W:3 * RWKV_W + LANES]
    gl = xs[:, 3 * RWKV_W + LANES:]

    lane = lax.broadcasted_iota(jnp.int32, (tm, LANES), 1)
    wa = jnp.where(lane < DECAY_LORA, jnp.tanh(wa), wa)
    wa_pre = _dot(wa.astype(bf16), wwa_ref[...])
    z = -(w0_ref[...] + wa_pre[:, :RWKV_W])
    softplus = jnp.maximum(z, 0.0) + jnp.log(1.0 + jnp.exp(-jnp.abs(z)))
    logd = -jnp.exp(-softplus - 0.5)
    a = jax.nn.sigmoid(a0_ref[...] + wa_pre[:, RWKV_W:])
    g = _dot(jax.nn.sigmoid(gl).astype(bf16), g2_ref[...])

    e64 = e64_ref[...]
    kk = k * kk_ref[...]
    kk = kk / jnp.maximum(jnp.sqrt(_dot((kk * kk).astype(bf16), e64)), 1e-12)
    kmod = k * (1.0 + (a - 1.0) * ka_ref[...])
    bonus = _dot((r * kmod * rk_ref[...]).astype(bf16), e64) * v

    cl = _dot_split(tri_ref[...], logd)
    cl3 = cl.reshape(tm // CHUNK, CHUNK, RWKV_W)
    cl_end = jnp.broadcast_to(cl3[:, CHUNK - 1:CHUNK, :], cl3.shape).reshape(tm, RWKV_W)
    at_scr[...] = (-kk * jnp.exp(cl - logd)).astype(bf16)
    rt_scr[...] = (r * jnp.exp(cl)).astype(bf16)
    pinv = jnp.exp(-cl)
    bt_scr[...] = (kk * a * pinv).astype(bf16)
    kt_scr[...] = (kmod * pinv).astype(bf16)
    q_end = jnp.exp(cl_end - cl)
    bp_scr[...] = (kk * a * q_end).astype(bf16)
    kp_scr[...] = (kmod * q_end).astype(bf16)
    v_scr[...] = v.astype(bf16)
    pc_scr[...] = jnp.exp(cl_end)

    c2 = 2 * CHUNK
    lane_c = lax.broadcasted_iota(jnp.int32, (CHUNK, c2), 1)
    row_c = lax.broadcasted_iota(jnp.int32, (CHUNK, c2), 0)
    left = lane_c < CHUNK
    s_idx = jnp.where(left, lane_c, lane_c - CHUNK)
    strict = s_idx < row_c
    incl = s_idx <= row_c
    eye_sbs = (s_idx == row_c).astype(f32)
    level_masks = [((row_c >> (lvl + 1)) == (s_idx >> (lvl + 1))) & (((row_c >> lvl) & 1) == 1)
                   & (((s_idx >> lvl) & 1) == 0) for lvl in range(6)]
    bd_mask = ((lax.broadcasted_iota(jnp.int32, (c2, c2), 0) // CHUNK)
               == (lax.broadcasted_iota(jnp.int32, (c2, c2), 1) // CHUNK))

    def stack2(x):
        zero = jnp.zeros_like(x)
        return jnp.concatenate([jnp.where(left, x, zero), jnp.where(left, zero, x)], axis=0)

    n_chunk = tm // CHUNK
    units = [(c, p) for c in range(n_chunk) for p in range(N_PAIR)]

    def region(c, p):
        return slice(c * CHUNK, (c + 1) * CHUNK), slice(p * LANES, (p + 1) * LANES)

    ar, vb, vst, a_ab, a_ak, a_r, t_inv, w1 = {}, {}, {}, {}, {}, {}, {}, {}
    for u_ in units:
        rows, cols = region(*u_)
        ar[u_] = jnp.concatenate([at_scr[rows, cols], rt_scr[rows, cols]], axis=0)
        yk = jnp.concatenate([stack2(bt_scr[rows, cols]), stack2(kt_scr[rows, cols])], axis=0)
        gm = _dot_nt(ar[u_], yk)
        a_ab[u_] = gm[:CHUNK, :c2]
        a_ak[u_] = jnp.where(strict, gm[:CHUNK, c2:], 0.0).astype(bf16)
        a_r[u_] = jnp.concatenate([jnp.where(incl, gm[CHUNK:, :c2], 0.0),
                                   jnp.where(incl, gm[CHUNK:, c2:], 0.0)], axis=1).astype(bf16)
        t_inv[u_] = eye_sbs + jnp.where(level_masks[0], a_ab[u_], 0.0)
    for lvl in range(1, 6):
        nt = {}
        for u_ in units:
            n_k = jnp.where(level_masks[lvl], a_ab[u_], 0.0).astype(bf16)
            nt[u_] = _dot(n_k, stack2(t_inv[u_].astype(bf16)))
        for u_ in units:
            t_inv[u_] = t_inv[u_] + _dot(t_inv[u_].astype(bf16), stack2(nt[u_].astype(bf16)))
    for u_ in units:
        rows, cols = region(*u_)
        vb[u_] = v_scr[rows, cols]
        vst[u_] = stack2(vb[u_])
        w1[u_] = _dot(a_ak[u_], vst[u_])
    a2, u1 = {}, {}
    for u_ in units:
        au = _dot(t_inv[u_].astype(bf16),
                  jnp.concatenate([stack2(ar[u_][:CHUNK]), stack2(w1[u_].astype(bf16))], axis=1))
        a2[u_] = au[:, :c2].astype(bf16)
        u1[u_] = au[:, c2:].astype(bf16)
    m_c, n_c, rt2, y1 = {}, {}, {}, {}
    for u_ in units:
        rows, cols = region(*u_)
        lhs = jnp.concatenate([jnp.concatenate([a2[u_], u1[u_]], axis=1),
                               jnp.concatenate([jnp.zeros_like(vb[u_]), vb[u_]], axis=1)], axis=0)
        mn = _dot_tn(lhs, jnp.concatenate([bp_scr[rows, cols], kp_scr[rows, cols]], axis=0))
        m_c[u_] = jnp.where(bd_mask, mn[:c2], 0.0).astype(bf16)
        n_c[u_] = jnp.where(bd_mask, mn[c2:], 0.0)
    for u_ in units:
        rt2[u_] = (ar[u_][CHUNK:].astype(f32) + _dot(a_r[u_][:, :c2], stack2(a2[u_]))).astype(bf16)
    for u_ in units:
        y1[u_] = _dot(a_r[u_], jnp.concatenate([stack2(u1[u_]), vst[u_]], axis=0))

    ys = [[None] * N_PAIR for _ in range(n_chunk)]
    ht = [state_scr[p] for p in range(N_PAIR)]
    for c in range(n_chunk):
        hb = [ht[p].astype(bf16) for p in range(N_PAIR)]
        for p in range(N_PAIR):
            cols = region(c, p)[1]
            ht[p] = ht[p] * pc_scr[c * CHUNK:c * CHUNK + 1, cols] + _dot(hb[p], m_c[c, p]) + n_c[c, p]
        for p in range(N_PAIR):
            ys[c][p] = _dot_nt(rt2[c, p], hb[p]) + y1[c, p]
    for p in range(N_PAIR):
        state_scr[p] = ht[p]

    y = jnp.concatenate([jnp.concatenate(row_y, axis=1) for row_y in ys], axis=0)
    inv_n = 1.0 / RWKV_HEAD
    mean = _dot(y.astype(bf16), e64) * inv_n
    yc = y - mean
    var = _dot((yc * yc).astype(bf16), e64) * inv_n
    yn = yc * lax.rsqrt(var + GN_EPS) * gng_ref[...] + gnb_ref[...]
    y_ref[...] = (yn + bonus) * g


def _const_spec(shape):
    return pl.BlockSpec(shape, lambda *_: (0,) * len(shape), pipeline_mode=pl.Buffered(1))


def _layer_spec(stacked, layer):
    rest = stacked.shape[1:]
    return pl.BlockSpec((None,) + rest, lambda *_: (layer,) + (0,) * len(rest), pipeline_mode=pl.Buffered(1))


def _params(sem):
    return pltpu.CompilerParams(dimension_semantics=sem, vmem_limit_bytes=VMEM_LIMIT)


def _ffn_in(x, g1, wg, wu, wd, gm, win, layer):
    t = x.shape[0]
    tm = TM_FFN
    row = lambda w: pl.BlockSpec((tm, w), lambda i: (i, 0))
    return pl.pallas_call(
        _ffn_in_kernel,
        grid=(t // tm,),
        in_specs=[row(D_MODEL), _const_spec(g1.shape), _layer_spec(wg, layer), _layer_spec(wu, layer),
                  _layer_spec(wd, layer), _const_spec(gm.shape), _layer_spec(win, layer)],
        out_specs=[row(D_MODEL), row(P_MLA_W), row(P_RWKV_W)],
        out_shape=[jax.ShapeDtypeStruct((t, D_MODEL), f32), jax.ShapeDtypeStruct((t, P_MLA_W), f32),
                   jax.ShapeDtypeStruct((t, P_RWKV_W), f32)],
        compiler_params=_params(("parallel",)),
        name="ffn_in",
    )(x, g1, wg, wu, wd, gm, win)


def _out_ffn(x1, o, y, go, gy, wo, g2, wg, wu, wd, layer):
    t = x1.shape[0]
    tm = TM_FFN
    row = lambda w: pl.BlockSpec((tm, w), lambda i: (i, 0))
    return pl.pallas_call(
        _out_ffn_kernel,
        grid=(t // tm,),
        in_specs=[row(D_MODEL), row(MLA_W), row(RWKV_W), _const_spec(go.shape), _const_spec(gy.shape),
                  _layer_spec(wo, layer), _const_spec(g2.shape), _layer_spec(wg, layer), _layer_spec(wu, layer),
                  _layer_spec(wd, layer)],
        out_specs=row(D_MODEL),
        out_shape=jax.ShapeDtypeStruct((t, D_MODEL), f32),
        compiler_params=_params(("parallel",)),
        name="out_ffn",
    )(x1, o, y, go, gy, wo, g2, wg, wu, wd)


def _mla_prep(pm, cos_t, sin_t, gq, gkv, wq, wk, wvt, qn, kn, seq):
    t = pm.shape[0]
    tm = TM_PREP
    n_pos = seq // tm
    row = lambda w: pl.BlockSpec((tm, w), lambda i: (i, 0))
    pos = pl.BlockSpec((tm, LANES), lambda i: (i % n_pos, 0))
    return pl.pallas_call(
        _mla_prep_kernel,
        grid=(t // tm,),
        in_specs=[row(P_MLA_W), pos, pos, _const_spec(gq.shape), _const_spec(gkv.shape), _const_spec(wq.shape),
                  _const_spec(wk.shape), _const_spec(wvt.shape), _const_spec(qn.shape), _const_spec(kn.shape)],
        out_specs=[row(QK_PAD_W), row(QK_PAD_W), pl.BlockSpec((MLA_W, tm), lambda i: (0, i))],
        out_shape=[jax.ShapeDtypeStruct((t, QK_PAD_W), bf16), jax.ShapeDtypeStruct((t, QK_PAD_W), bf16),
                   jax.ShapeDtypeStruct((MLA_W, t), bf16)],
        compiler_params=_params(("parallel",)),
        name="mla_prep",
    )(pm, cos_t, sin_t, gq, gkv, wq, wk, wvt, qn, kn)


def _attention(q, k, vt, batch, seq):
    return pl.pallas_call(
        _attn_kernel,
        grid=(batch, MLA_HEADS // 2),
        in_specs=[pl.BlockSpec((seq, 2 * HEAD_PAD), lambda b, hp: (b, hp)),
                  pl.BlockSpec((seq, 2 * HEAD_PAD), lambda b, hp: (b, hp)),
                  pl.BlockSpec((2 * MLA_V, seq), lambda b, hp: (hp, b))],
        out_specs=pl.BlockSpec((seq, 2 * MLA_V), lambda b, hp: (b, hp)),
        out_shape=jax.ShapeDtypeStruct((batch * seq, MLA_W), f32),
        scratch_shapes=[pltpu.VMEM((2, TQ, TQ), f32)] * 2,
        compiler_params=_params(("parallel", "parallel")),
        name="attention",
    )(q, k, vt)


def _rwkv(pr, mu, w0, a0, wwa, g2, kk, ka, rk, gng, gnb, e64, tri, batch, seq):
    tm = TM_RWKV
    ns = seq // tm
    vecs = [mu, w0, a0, wwa, g2, kk, ka, rk, gng, gnb, e64, tri]
    return pl.pallas_call(
        _rwkv_kernel,
        grid=(batch, ns),
        in_specs=[pl.BlockSpec((tm, P_RWKV_W), lambda b, i: (b * ns + i, 0))] + [_const_spec(a.shape) for a in vecs],
        out_specs=pl.BlockSpec((tm, RWKV_W), lambda b, i: (b * ns + i, 0)),
        out_shape=jax.ShapeDtypeStruct((batch * seq, RWKV_W), f32),
        scratch_shapes=[pltpu.VMEM((tm + 8, P_RWKV_W), f32),
                        pltpu.VMEM((N_PAIR, LANES, LANES), f32)]
                       + [pltpu.VMEM((tm, RWKV_W), bf16)] * 7
                       + [pltpu.VMEM((tm, RWKV_W), f32)],
        compiler_params=_params(("arbitrary", "arbitrary")),
        name="rwkv",
    )(pr, *vecs)


def _win_ext(w_in):
    z = lambda n: jnp.zeros(w_in.shape[:-1] + (n,), w_in.dtype)
    pe = w_in[..., Q_LORA + KV_LORA:MLA_IN]
    half = MLA_ROPE // 2
    rw = w_in[..., MLA_IN:]
    o = 0
    r_w = rw[..., o:o + RWKV_W]; o += RWKV_W
    wl_w = rw[..., o:o + DECAY_LORA]; o += DECAY_LORA
    k_w = rw[..., o:o + RWKV_W]; o += RWKV_W
    v_w = rw[..., o:o + RWKV_W]; o += RWKV_W
    al_w = rw[..., o:o + A_LORA]; o += A_LORA
    gl_w = rw[..., o:o + GATE_LORA]
    return jnp.concatenate([
        w_in[..., :Q_LORA + KV_LORA],
        z(MLA_NOPE), pe, z(LANES - MLA_QK),
        z(MLA_NOPE), pe[..., half:], pe[..., :half], z(LANES - MLA_QK),
        r_w, k_w, v_w, wl_w, al_w, gl_w, z(GATE_PAD - GATE_LORA)], axis=-1)


def _mu_ext(mu):
    o = 0
    r_m = mu[o:o + RWKV_W]; o += RWKV_W
    wl_m = mu[o:o + DECAY_LORA]; o += DECAY_LORA
    k_m = mu[o:o + RWKV_W]; o += RWKV_W
    v_m = mu[o:o + RWKV_W]; o += RWKV_W
    al_m = mu[o:o + A_LORA]; o += A_LORA
    gl_m = mu[o:o + GATE_LORA]
    return jnp.concatenate([r_m, k_m, v_m, wl_m, al_m, gl_m, jnp.zeros((GATE_PAD - GATE_LORA,), mu.dtype)])[None, :]


def _head_pad_cols(w, width):
    kdim = w.shape[0]
    w = w.reshape(kdim, MLA_HEADS, width)
    return jnp.pad(w, ((0, 0), (0, 0), (0, HEAD_PAD - width))).reshape(kdim, QK_PAD_W)


def _swap_rope_cols(w):
    kdim = w.shape[0]
    half = MLA_ROPE // 2
    w = w.reshape(kdim, MLA_HEADS, MLA_QK)
    sw = jnp.concatenate([jnp.zeros((kdim, MLA_HEADS, MLA_NOPE), w.dtype), w[..., MLA_NOPE + half:],
                          w[..., MLA_NOPE:MLA_NOPE + half],
                          jnp.zeros((kdim, MLA_HEADS, HEAD_PAD - MLA_QK), w.dtype)], axis=-1)
    return sw.reshape(kdim, QK_PAD_W)


def _norm_pair(g):
    half = MLA_ROPE // 2
    zpad = jnp.zeros((HEAD_PAD - MLA_QK,), g.dtype)
    plain = jnp.concatenate([g, zpad])
    sw = jnp.concatenate([jnp.zeros((MLA_NOPE,), g.dtype), g[MLA_NOPE + half:], g[MLA_NOPE:MLA_NOPE + half], zpad])
    return jnp.stack([plain, sw])


def _rope_tiles(seq):
    pos = jnp.arange(seq, dtype=f32)
    inv_freq = ROPE_BASE ** (-jnp.arange(0, MLA_ROPE, 2, dtype=f32) / MLA_ROPE)
    ang = pos[:, None] * inv_freq[None, :]
    cos, sin = jnp.cos(ang), jnp.sin(ang)
    zpad = jnp.zeros((seq, HEAD_PAD - MLA_QK), f32)
    cos_t = jnp.concatenate([jnp.ones((seq, MLA_NOPE), f32), cos, cos, zpad], axis=1)
    sin_t = jnp.concatenate([jnp.zeros((seq, MLA_NOPE), f32), -sin, sin, zpad], axis=1)
    return cos_t, sin_t


def _block_diag_const(n, blk, lower):
    r = jnp.arange(n)[:, None]
    c = jnp.arange(n)[None, :]
    m = (r // blk) == (c // blk)
    if lower:
        m = m & (c <= r)
    return m.astype(bf16)


@jax.jit
def kernel(x, ffn1_norm, ffn1_w_gate, ffn1_w_up, ffn1_w_down, mix_norm, w_in, q_lat_norm, w_q_up, kv_lat_norm, w_kv_up, q_norm, k_norm, shift_mu, w0, w2, a0, a2, g2, k_k, k_a, r_k, gn_gain, gn_bias, out_norm_mla, out_norm_rwkv, w_out, ffn2_norm, ffn2_w_gate, ffn2_w_up, ffn2_w_down):
    batch, seq, d = x.shape
    xt = x.reshape(batch * seq, d)
    cos_t, sin_t = _rope_tiles(seq)
    e64 = _block_diag_const(RWKV_W, RWKV_HEAD, lower=False)
    tri = _block_diag_const(TM_RWKV, CHUNK, lower=True)
    row = lambda a: a[None, :]
    wg1, wu1, wd1 = ffn1_w_gate.astype(bf16), ffn1_w_up.astype(bf16), ffn1_w_down.astype(bf16)
    wg2, wu2, wd2 = ffn2_w_gate.astype(bf16), ffn2_w_up.astype(bf16), ffn2_w_down.astype(bf16)
    win = _win_ext(w_in.astype(bf16))
    wo = w_out.astype(bf16)
    for l in range(DEPTH):
        xt, pm, pr = _ffn_in(xt, row(ffn1_norm[l]), wg1, wu1, wd1, row(mix_norm[l]), win, l)
        wq = jnp.concatenate([_head_pad_cols(w_q_up[l], MLA_QK), _swap_rope_cols(w_q_up[l])], axis=1).astype(bf16)
        wkv3 = w_kv_up[l].reshape(KV_LORA, MLA_HEADS, MLA_NOPE + MLA_V)
        wk = _head_pad_cols(wkv3[..., :MLA_NOPE].reshape(KV_LORA, -1), MLA_NOPE).astype(bf16)
        wvt = wkv3[..., MLA_NOPE:].reshape(KV_LORA, MLA_W).T.astype(bf16)
        q, k, vt = _mla_prep(pm, cos_t, sin_t, row(q_lat_norm[l]), row(kv_lat_norm[l]), wq, wk, wvt,
                             _norm_pair(q_norm[l]), _norm_pair(k_norm[l]), seq)
        o = _attention(q, k, vt, batch, seq)
        zl = jnp.zeros((DECAY_LORA, RWKV_W), f32)
        wwa = jnp.concatenate([jnp.concatenate([w2[l], zl], axis=1), jnp.concatenate([zl, a2[l]], axis=1)], axis=0)
        g2p = jnp.pad(g2[l], ((0, GATE_PAD - GATE_LORA), (0, 0)))
        y = _rwkv(pr, _mu_ext(shift_mu[l]), row(w0[l]), row(a0[l]), wwa.astype(bf16), g2p.astype(bf16),
                  row(k_k[l]), row(k_a[l]), row(r_k[l].reshape(-1)), row(gn_gain[l]), row(gn_bias[l]),
                  e64, tri, batch, seq)
        xt = _out_ffn(xt, o, y, row(out_norm_mla[l]), row(out_norm_rwkv[l]), wo, row(ffn2_norm[l]), wg2, wu2, wd2, l)
    return xt.reshape(batch, seq, d)
```

```python
import jax
import jax.numpy as jnp
from jax import lax
from jax.experimental import pallas as pl
from jax.experimental.pallas import tpu as pltpu

f32 = jnp.float32
bf16 = jnp.bfloat16

D_MODEL = 1024
DEPTH = 4
CHUNK = 64
D_FF = 2816
EPS = 1e-6
MLA_HEADS = 8
MLA_NOPE = 64
MLA_ROPE = 32
MLA_QK = MLA_NOPE + MLA_ROPE
MLA_V = 64
Q_LORA = 256
KV_LORA = 128
ROPE_BASE = 10000.0
RWKV_HEADS = 8
RWKV_HEAD = 64
RWKV_W = RWKV_HEADS * RWKV_HEAD
DECAY_LORA = 64
A_LORA = 64
GATE_LORA = 160
GN_EPS = 64e-5
MLA_W = MLA_HEADS * MLA_V
MLA_IN = Q_LORA + KV_LORA + MLA_ROPE

LANES = 128
HEAD_PAD = LANES
QK_PAD_W = MLA_HEADS * HEAD_PAD
P_MLA_W = Q_LORA + KV_LORA + 2 * LANES
GATE_PAD = 2 * LANES
P_RWKV_W = 3 * RWKV_W + LANES + GATE_PAD
VMEM_LIMIT = 56 * 1024 * 1024

TM_FFN = 256
TM_PREP = 512
TQ = 512
HALF_TQ = TQ // 2
ONES_ROWS = 16
LOG2_E = 1.4426950408889634
TM_RWKV = 1024
TRI_ROWS = 256
N_PAIR = RWKV_HEADS // 2


def _rms(x, g):
    return x * lax.rsqrt(jnp.mean(x * x, axis=-1, keepdims=True) + EPS) * g


def _dot(a, b):
    return jnp.dot(a, b, preferred_element_type=f32)


def _dot_nt(a, b):
    return lax.dot_general(a, b, (((1,), (1,)), ((), ())), preferred_element_type=f32)


def _dot_tn(a, b):
    return lax.dot_general(a, b, (((0,), (0,)), ((), ())), preferred_element_type=f32)


def _dot_split(c, x):
    hi = x.astype(bf16)
    lo = (x - hi.astype(f32)).astype(bf16)
    return _dot(c, hi) + _dot(c, lo)


def _swiglu_half(x, g, wg_ref, wu_ref, wd_ref):
    h = _rms(x, g).astype(bf16)
    gate = _dot(h, wg_ref[...])
    up = _dot(h, wu_ref[...])
    act = (jax.nn.silu(gate) * up).astype(bf16)
    return x + 0.5 * _dot(act, wd_ref[...])


def _ffn_in_kernel(x_ref, g1_ref, wg_ref, wu_ref, wd_ref, gm_ref, win_ref, x1_ref, pm_ref, pr_ref):
    x1 = _swiglu_half(x_ref[...], g1_ref[...], wg_ref, wu_ref, wd_ref)
    x1_ref[...] = x1
    p = _dot(_rms(x1, gm_ref[...]).astype(bf16), win_ref[...])
    pm_ref[...] = p[:, :P_MLA_W]
    pr_ref[...] = p[:, P_MLA_W:]


def _out_ffn_kernel(x_ref, o_ref, y_ref, go_ref, gy_ref, wo_ref, g2_ref, wg_ref, wu_ref, wd_ref, x3_ref):
    ho = _rms(o_ref[...], go_ref[...]).astype(bf16)
    hy = _rms(y_ref[...], gy_ref[...]).astype(bf16)
    x2 = x_ref[...] + _dot(ho, wo_ref[:MLA_W, :]) + _dot(hy, wo_ref[MLA_W:, :])
    x3_ref[...] = _swiglu_half(x2, g2_ref[...], wg_ref, wu_ref, wd_ref)


def _mla_prep_kernel(pm_ref, cos_ref, sin_ref, gq_ref, gkv_ref, wq_ref, wk_ref, wvt_ref, qn_ref, kn_ref,
                     q_ref, k_ref, vt_ref):
    pm = pm_ref[...]
    q_lat = pm[:, :Q_LORA]
    kv_lat = pm[:, Q_LORA:Q_LORA + KV_LORA]
    kpe = pm[:, Q_LORA + KV_LORA:Q_LORA + KV_LORA + LANES]
    kpe_sw = pm[:, Q_LORA + KV_LORA + LANES:]
    cos_t = cos_ref[...]
    sin_t = sin_ref[...]
    q2 = _dot(_rms(q_lat, gq_ref[...]).astype(bf16), wq_ref[...])
    hkv = _rms(kv_lat, gkv_ref[...]).astype(bf16)
    kv = _dot(hkv, wk_ref[...])
    vt_ref[...] = _dot_nt(wvt_ref[...], hkv).astype(bf16)
    qc = qn_ref[0:1, :] * cos_t
    qs = qn_ref[1:2, :] * sin_t
    kc = kn_ref[0:1, :] * cos_t
    ks = kn_ref[1:2, :] * sin_t
    scale = MLA_QK ** -0.5 * LOG2_E
    for h in range(MLA_HEADS):
        sl = slice(h * HEAD_PAD, (h + 1) * HEAD_PAD)
        qh = q2[:, sl]
        qsw = q2[:, QK_PAD_W + h * HEAD_PAD:QK_PAD_W + (h + 1) * HEAD_PAD]
        rs = lax.rsqrt(jnp.sum(qh * qh, axis=-1, keepdims=True) * (1.0 / MLA_QK) + EPS)
        q_ref[:, sl] = ((qh * qc + qsw * qs) * (rs * scale)).astype(bf16)
        kh = kv[:, sl] + kpe
        rk = lax.rsqrt(jnp.sum(kh * kh, axis=-1, keepdims=True) * (1.0 / MLA_QK) + EPS)
        k_ref[:, sl] = ((kh * kc + kpe_sw * ks) * rk).astype(bf16)


def _attn_kernel(q_ref, k_ref, vt_ref, o_ref, s0_scr, s1_scr):
    n_tiles = q_ref.shape[0] // TQ
    chunk_gap = (lax.broadcasted_iota(jnp.int32, (HALF_TQ, HALF_TQ), 0) // CHUNK
                 - lax.broadcasted_iota(jnp.int32, (HALF_TQ, HALF_TQ), 1) // CHUNK)
    diag_visible = chunk_gap <= 0
    heads = [slice(h * HEAD_PAD, (h + 1) * HEAD_PAD) for h in range(2)]
    ones_rows = jnp.ones((ONES_ROWS, TQ), bf16)
    bufs = (s0_scr, s1_scr)
    tile = lambda n: slice(n * TQ, (n + 1) * TQ)
    lo_half = lambda n: slice(n * TQ, n * TQ + HALF_TQ)
    hi_half = lambda n: slice(n * TQ + HALF_TQ, (n + 1) * TQ)

    def scores(step, s_scr):
        i, j = step
        col_max = []
        for h in range(2):
            if j == i:
                top = _dot_nt(k_ref[lo_half(j), heads[h]], q_ref[tile(i), heads[h]])
                top = jnp.concatenate([jnp.where(diag_visible, top[:, :HALF_TQ], -jnp.inf), top[:, HALF_TQ:]], axis=1)
                bot = _dot_nt(k_ref[hi_half(j), heads[h]], q_ref[hi_half(i), heads[h]])
                bot = jnp.where(diag_visible, bot, -jnp.inf)
                s_scr[h, :HALF_TQ, :] = top
                s_scr[h, HALF_TQ:, HALF_TQ:] = bot
                top_max = jnp.max(top, axis=0, keepdims=True)
                col_max.append(jnp.concatenate(
                    [top_max[:, :HALF_TQ], jnp.maximum(top_max[:, HALF_TQ:], jnp.max(bot, axis=0, keepdims=True))],
                    axis=1))
            else:
                st = _dot_nt(k_ref[tile(j), heads[h]], q_ref[tile(i), heads[h]])
                s_scr[h] = st
                col_max.append(jnp.max(st, axis=0, keepdims=True))
        return col_max

    def vt_aug(h, cols):
        return jnp.concatenate([vt_ref[h * MLA_V:(h + 1) * MLA_V, cols], ones_rows[:, :cols.stop - cols.start]], axis=0)

    def consume(step, s_scr, col_max, stats):
        i, j = step
        new = []
        for h in range(2):
            m, acc = stats[h]
            m_new = jnp.maximum(m, col_max[h])
            alpha = jnp.exp2(m - m_new)
            if j == i:
                p_top = jnp.exp2(s_scr[h, :HALF_TQ, :] - m_new).astype(bf16)
                p_bot = jnp.exp2(s_scr[h, HALF_TQ:, HALF_TQ:] - m_new[:, HALF_TQ:]).astype(bf16)
                upd = _dot(vt_aug(h, lo_half(j)), p_top)
                upd_hi = _dot(vt_aug(h, hi_half(j)), p_bot)
                upd = jnp.concatenate([upd[:, :HALF_TQ], upd[:, HALF_TQ:] + upd_hi], axis=1)
            else:
                upd = _dot(vt_aug(h, tile(j)), jnp.exp2(s_scr[h] - m_new).astype(bf16))
            new.append((m_new, alpha * acc + upd))
        return tuple(new)

    steps = [(i, j) for i in range(n_tiles) for j in range(i + 1)]
    next_max = scores(steps[0], bufs[0])
    stats = None
    for n, step in enumerate(steps):
        i, j = step
        cur_max = next_max
        if n + 1 < len(steps):
            next_max = scores(steps[n + 1], bufs[(n + 1) % 2])
        if j == 0:
            stats = tuple((jnp.full((1, TQ), -jnp.inf, f32), jnp.zeros((MLA_V + ONES_ROWS, TQ), f32))
                          for _ in range(2))
        stats = consume(step, bufs[n % 2], cur_max, stats)
        if j == i:
            (_, acc0), (_, acc1) = stats
            o_ref[tile(i), :] = jnp.concatenate([acc0[:MLA_V] / acc0[MLA_V:MLA_V + 1],
                                                 acc1[:MLA_V] / acc1[MLA_V:MLA_V + 1]], axis=0).T


def _rwkv_kernel(p_ref, mu_ref, w0_ref, a0_ref, wwa_ref, g2_ref, kk_ref, ka_ref, rk_ref, gng_ref, gnb_ref,
                 e64_ref, tri_ref, y_ref,
                 shift_scr, state_scr, at_scr, rt_scr, bt_scr, kt_scr, bp_scr, kp_scr, v_scr, pc_scr):
    tm = TM_RWKV
    i = pl.program_id(1)

    @pl.when(i == 0)
    def _():
        shift_scr[0:8, :] = jnp.zeros((8, P_RWKV_W), f32)
        state_scr[...] = jnp.zeros_like(state_scr)

    cur = p_ref[...]
    shift_scr[8:8 + tm, :] = cur
    prev = shift_scr[7:7 + tm, :]
    xs = cur + mu_ref[...] * (prev - cur)
    shift_scr[0:8, :] = shift_scr[tm:tm + 8, :]

    r = xs[:, 0:RWKV_W]
    k = xs[:, RWKV_W:2 * RWKV_W]
    v = xs[:, 2 * RWKV_W:3 * RWKV_W]
    wa = xs[:, 3 * RWKV_W:3 * RWKV_W + LANES]
    gl = xs[:, 3 * RWKV_W + LANES:]

    lane = lax.broadcasted_iota(jnp.int32, (tm, LANES), 1)
    wa = jnp.where(lane < DECAY_LORA, jnp.tanh(wa), wa)
    wa_pre = _dot(wa.astype(bf16), wwa_ref[...])
    z = -(w0_ref[...] + wa_pre[:, :RWKV_W])
    softplus = jnp.maximum(z, 0.0) + jnp.log(1.0 + jnp.exp(-jnp.abs(z)))
    logd = -jnp.exp(-softplus - 0.5)
    a = jax.nn.sigmoid(a0_ref[...] + wa_pre[:, RWKV_W:])
    g = _dot(jax.nn.sigmoid(gl).astype(bf16), g2_ref[...])

    e64 = e64_ref[...]
    kk = k * kk_ref[...]
    kk = kk / jnp.maximum(jnp.sqrt(_dot((kk * kk).astype(bf16), e64)), 1e-12)
    kmod = k * (1.0 + (a - 1.0) * ka_ref[...])
    bonus = _dot((r * kmod * rk_ref[...]).astype(bf16), e64) * v

    tri_rows = tri_ref.shape[0]
    cl = jnp.concatenate([_dot_split(tri_ref[...], logd[s * tri_rows:(s + 1) * tri_rows])
                          for s in range(tm // tri_rows)], axis=0)
    cl3 = cl.reshape(tm // CHUNK, CHUNK, RWKV_W)
    cl_end = jnp.broadcast_to(cl3[:, CHUNK - 1:CHUNK, :], cl3.shape).reshape(tm, RWKV_W)
    at_scr[...] = (-kk * jnp.exp(cl - logd)).astype(bf16)
    rt_scr[...] = (r * jnp.exp(cl)).astype(bf16)
    pinv = jnp.exp(-cl)
    bt_scr[...] = (kk * a * pinv).astype(bf16)
    kt_scr[...] = (kmod * pinv).astype(bf16)
    chunk_decay = jnp.exp(cl_end)
    q_end = chunk_decay * pinv
    bp_scr[...] = (kk * a * q_end).astype(bf16)
    kp_scr[...] = (kmod * q_end).astype(bf16)
    v_scr[...] = v.astype(bf16)
    pc_scr[...] = chunk_decay

    c2 = 2 * CHUNK
    lane_c = lax.broadcasted_iota(jnp.int32, (CHUNK, c2), 1)
    row_c = lax.broadcasted_iota(jnp.int32, (CHUNK, c2), 0)
    left = lane_c < CHUNK
    s_idx = jnp.where(left, lane_c, lane_c - CHUNK)
    strict = s_idx < row_c
    incl = s_idx <= row_c
    eye_sbs = (s_idx == row_c).astype(f32)
    level_masks = [((row_c >> (lvl + 1)) == (s_idx >> (lvl + 1))) & (((row_c >> lvl) & 1) == 1)
                   & (((s_idx >> lvl) & 1) == 0) for lvl in range(6)]
    bd_mask = ((lax.broadcasted_iota(jnp.int32, (c2, c2), 0) // CHUNK)
               == (lax.broadcasted_iota(jnp.int32, (c2, c2), 1) // CHUNK))

    def stack2(x):
        zero = jnp.zeros_like(x)
        return jnp.concatenate([jnp.where(left, x, zero), jnp.where(left, zero, x)], axis=0)

    n_chunk = tm // CHUNK
    units = [(c, p) for c in range(n_chunk) for p in range(N_PAIR)]

    def region(c, p):
        return slice(c * CHUNK, (c + 1) * CHUNK), slice(p * LANES, (p + 1) * LANES)

    ar, vb, vst, a_ab, a_ak, a_r, t_inv, w1 = {}, {}, {}, {}, {}, {}, {}, {}
    for u_ in units:
        rows, cols = region(*u_)
        ar[u_] = jnp.concatenate([at_scr[rows, cols], rt_scr[rows, cols]], axis=0)
        yk = jnp.concatenate([stack2(bt_scr[rows, cols]), stack2(kt_scr[rows, cols])], axis=0)
        gm = _dot_nt(ar[u_], yk)
        a_ab[u_] = gm[:CHUNK, :c2]
        a_ak[u_] = jnp.where(strict, gm[:CHUNK, c2:], 0.0).astype(bf16)
        a_r[u_] = jnp.concatenate([jnp.where(incl, gm[CHUNK:, :c2], 0.0),
                                   jnp.where(incl, gm[CHUNK:, c2:], 0.0)], axis=1).astype(bf16)
        t_inv[u_] = eye_sbs + jnp.where(level_masks[0], a_ab[u_], 0.0)
    for lvl in range(1, 6):
        nt = {}
        for u_ in units:
            n_k = jnp.where(level_masks[lvl], a_ab[u_], 0.0).astype(bf16)
            nt[u_] = _dot(n_k, stack2(t_inv[u_].astype(bf16)))
        for u_ in units:
            t_inv[u_] = t_inv[u_] + _dot(t_inv[u_].astype(bf16), stack2(nt[u_].astype(bf16)))
    for u_ in units:
        rows, cols = region(*u_)
        vb[u_] = v_scr[rows, cols]
        vst[u_] = stack2(vb[u_])
        w1[u_] = _dot(a_ak[u_], vst[u_])
    a2, u1 = {}, {}
    for u_ in units:
        au = _dot(t_inv[u_].astype(bf16),
                  jnp.concatenate([stack2(ar[u_][:CHUNK]), stack2(w1[u_].astype(bf16))], axis=1))
        a2[u_] = au[:, :c2].astype(bf16)
        u1[u_] = au[:, c2:].astype(bf16)
    m_c, n_c, rt2, y1 = {}, {}, {}, {}
    for u_ in units:
        rows, cols = region(*u_)
        lhs = jnp.concatenate([jnp.concatenate([a2[u_], u1[u_]], axis=1),
                               jnp.concatenate([jnp.zeros_like(vb[u_]), vb[u_]], axis=1)], axis=0)
        mn = _dot_tn(lhs, jnp.concatenate([bp_scr[rows, cols], kp_scr[rows, cols]], axis=0))
        m_c[u_] = jnp.where(bd_mask, mn[:c2], 0.0).astype(bf16)
        n_c[u_] = jnp.where(bd_mask, mn[c2:], 0.0)
    for u_ in units:
        rt2[u_] = (ar[u_][CHUNK:].astype(f32) + _dot(a_r[u_][:, :c2], stack2(a2[u_]))).astype(bf16)
    for u_ in units:
        y1[u_] = _dot(a_r[u_], jnp.concatenate([stack2(u1[u_]), vst[u_]], axis=0))

    ys = [[None] * N_PAIR for _ in range(n_chunk)]
    ht = [state_scr[p] for p in range(N_PAIR)]
    for c in range(n_chunk):
        hb = [ht[p].astype(bf16) for p in range(N_PAIR)]
        for p in range(N_PAIR):
            cols = region(c, p)[1]
            ht[p] = ht[p] * pc_scr[c * CHUNK:c * CHUNK + 1, cols] + _dot(hb[p], m_c[c, p]) + n_c[c, p]
        for p in range(N_PAIR):
            ys[c][p] = _dot_nt(rt2[c, p], hb[p]) + y1[c, p]
    for p in range(N_PAIR):
        state_scr[p] = ht[p]

    y = jnp.concatenate([jnp.concatenate(row_y, axis=1) for row_y in ys], axis=0)
    inv_n = 1.0 / RWKV_HEAD
    mean = _dot(y.astype(bf16), e64) * inv_n
    yc = y - mean
    var = _dot((yc * yc).astype(bf16), e64) * inv_n
    yn = yc * lax.rsqrt(var + GN_EPS) * gng_ref[...] + gnb_ref[...]
    y_ref[...] = (yn + bonus) * g


def _const_spec(shape):
    return pl.BlockSpec(shape, lambda *_: (0,) * len(shape), pipeline_mode=pl.Buffered(1))


def _layer_spec(stacked, layer):
    rest = stacked.shape[1:]
    return pl.BlockSpec((None,) + rest, lambda *_: (layer,) + (0,) * len(rest), pipeline_mode=pl.Buffered(1))


def _params(sem):
    return pltpu.CompilerParams(dimension_semantics=sem, vmem_limit_bytes=VMEM_LIMIT)


def _ffn_in(x, g1, wg, wu, wd, gm, win, layer):
    t = x.shape[0]
    tm = TM_FFN
    row = lambda w: pl.BlockSpec((tm, w), lambda i: (i, 0))
    return pl.pallas_call(
        _ffn_in_kernel,
        grid=(t // tm,),
        in_specs=[row(D_MODEL), _const_spec(g1.shape), _layer_spec(wg, layer), _layer_spec(wu, layer),
                  _layer_spec(wd, layer), _const_spec(gm.shape), _layer_spec(win, layer)],
        out_specs=[row(D_MODEL), row(P_MLA_W), row(P_RWKV_W)],
        out_shape=[jax.ShapeDtypeStruct((t, D_MODEL), f32), jax.ShapeDtypeStruct((t, P_MLA_W), f32),
                   jax.ShapeDtypeStruct((t, P_RWKV_W), f32)],
        compiler_params=_params(("parallel",)),
        name="ffn_in",
    )(x, g1, wg, wu, wd, gm, win)


def _out_ffn(x1, o, y, go, gy, wo, g2, wg, wu, wd, layer):
    t = x1.shape[0]
    tm = TM_FFN
    row = lambda w: pl.BlockSpec((tm, w), lambda i: (i, 0))
    return pl.pallas_call(
        _out_ffn_kernel,
        grid=(t // tm,),
        in_specs=[row(D_MODEL), row(MLA_W), row(RWKV_W), _const_spec(go.shape), _const_spec(gy.shape),
                  _layer_spec(wo, layer), _const_spec(g2.shape), _layer_spec(wg, layer), _layer_spec(wu, layer),
                  _layer_spec(wd, layer)],
        out_specs=row(D_MODEL),
        out_shape=jax.ShapeDtypeStruct((t, D_MODEL), f32),
        compiler_params=_params(("parallel",)),
        name="out_ffn",
    )(x1, o, y, go, gy, wo, g2, wg, wu, wd)


def _mla_prep(pm, cos_t, sin_t, gq, gkv, wq, wk, wvt, qn, kn, seq):
    t = pm.shape[0]
    tm = TM_PREP
    n_pos = seq // tm
    row = lambda w: pl.BlockSpec((tm, w), lambda i: (i, 0))
    pos = pl.BlockSpec((tm, LANES), lambda i: (i % n_pos, 0))
    return pl.pallas_call(
        _mla_prep_kernel,
        grid=(t // tm,),
        in_specs=[row(P_MLA_W), pos, pos, _const_spec(gq.shape), _const_spec(gkv.shape), _const_spec(wq.shape),
                  _const_spec(wk.shape), _const_spec(wvt.shape), _const_spec(qn.shape), _const_spec(kn.shape)],
        out_specs=[row(QK_PAD_W), row(QK_PAD_W), pl.BlockSpec((MLA_W, tm), lambda i: (0, i))],
        out_shape=[jax.ShapeDtypeStruct((t, QK_PAD_W), bf16), jax.ShapeDtypeStruct((t, QK_PAD_W), bf16),
                   jax.ShapeDtypeStruct((MLA_W, t), bf16)],
        compiler_params=_params(("parallel",)),
        name="mla_prep",
    )(pm, cos_t, sin_t, gq, gkv, wq, wk, wvt, qn, kn)


def _attention(q, k, vt, batch, seq):
    return pl.pallas_call(
        _attn_kernel,
        grid=(batch, MLA_HEADS // 2),
        in_specs=[pl.BlockSpec((seq, 2 * HEAD_PAD), lambda b, hp: (b, hp)),
                  pl.BlockSpec((seq, 2 * HEAD_PAD), lambda b, hp: (b, hp)),
                  pl.BlockSpec((2 * MLA_V, seq), lambda b, hp: (hp, b))],
        out_specs=pl.BlockSpec((seq, 2 * MLA_V), lambda b, hp: (b, hp)),
        out_shape=jax.ShapeDtypeStruct((batch * seq, MLA_W), f32),
        scratch_shapes=[pltpu.VMEM((2, TQ, TQ), f32)] * 2,
        compiler_params=_params(("parallel", "parallel")),
        name="attention",
    )(q, k, vt)


def _rwkv(pr, mu, w0, a0, wwa, g2, kk, ka, rk, gng, gnb, e64, tri, batch, seq):
    tm = TM_RWKV
    ns = seq // tm
    vecs = [mu, w0, a0, wwa, g2, kk, ka, rk, gng, gnb, e64, tri]
    return pl.pallas_call(
        _rwkv_kernel,
        grid=(batch, ns),
        in_specs=[pl.BlockSpec((tm, P_RWKV_W), lambda b, i: (b * ns + i, 0))] + [_const_spec(a.shape) for a in vecs],
        out_specs=pl.BlockSpec((tm, RWKV_W), lambda b, i: (b * ns + i, 0)),
        out_shape=jax.ShapeDtypeStruct((batch * seq, RWKV_W), f32),
        scratch_shapes=[pltpu.VMEM((tm + 8, P_RWKV_W), f32),
                        pltpu.VMEM((N_PAIR, LANES, LANES), f32)]
                       + [pltpu.VMEM((tm, RWKV_W), bf16)] * 7
                       + [pltpu.VMEM((tm, RWKV_W), f32)],
        compiler_params=_params(("arbitrary", "arbitrary")),
        name="rwkv",
    )(pr, *vecs)


def _win_ext(w_in):
    z = lambda n: jnp.zeros(w_in.shape[:-1] + (n,), w_in.dtype)
    pe = w_in[..., Q_LORA + KV_LORA:MLA_IN]
    half = MLA_ROPE // 2
    rw = w_in[..., MLA_IN:]
    o = 0
    r_w = rw[..., o:o + RWKV_W]; o += RWKV_W
    wl_w = rw[..., o:o + DECAY_LORA]; o += DECAY_LORA
    k_w = rw[..., o:o + RWKV_W]; o += RWKV_W
    v_w = rw[..., o:o + RWKV_W]; o += RWKV_W
    al_w = rw[..., o:o + A_LORA]; o += A_LORA
    gl_w = rw[..., o:o + GATE_LORA]
    return jnp.concatenate([
        w_in[..., :Q_LORA + KV_LORA],
        z(MLA_NOPE), pe, z(LANES - MLA_QK),
        z(MLA_NOPE), pe[..., half:], pe[..., :half], z(LANES - MLA_QK),
        r_w, k_w, v_w, wl_w, al_w, gl_w, z(GATE_PAD - GATE_LORA)], axis=-1)


def _mu_ext(mu):
    o = 0
    r_m = mu[o:o + RWKV_W]; o += RWKV_W
    wl_m = mu[o:o + DECAY_LORA]; o += DECAY_LORA
    k_m = mu[o:o + RWKV_W]; o += RWKV_W
    v_m = mu[o:o + RWKV_W]; o += RWKV_W
    al_m = mu[o:o + A_LORA]; o += A_LORA
    gl_m = mu[o:o + GATE_LORA]
    return jnp.concatenate([r_m, k_m, v_m, wl_m, al_m, gl_m, jnp.zeros((GATE_PAD - GATE_LORA,), mu.dtype)])[None, :]


def _head_pad_cols(w, width):
    kdim = w.shape[0]
    w = w.reshape(kdim, MLA_HEADS, width)
    return jnp.pad(w, ((0, 0), (0, 0), (0, HEAD_PAD - width))).reshape(kdim, QK_PAD_W)


def _swap_rope_cols(w):
    kdim = w.shape[0]
    half = MLA_ROPE // 2
    w = w.reshape(kdim, MLA_HEADS, MLA_QK)
    sw = jnp.concatenate([jnp.zeros((kdim, MLA_HEADS, MLA_NOPE), w.dtype), w[..., MLA_NOPE + half:],
                          w[..., MLA_NOPE:MLA_NOPE + half],
                          jnp.zeros((kdim, MLA_HEADS, HEAD_PAD - MLA_QK), w.dtype)], axis=-1)
    return sw.reshape(kdim, QK_PAD_W)


def _norm_pair(g):
    half = MLA_ROPE // 2
    zpad = jnp.zeros((HEAD_PAD - MLA_QK,), g.dtype)
    plain = jnp.concatenate([g, zpad])
    sw = jnp.concatenate([jnp.zeros((MLA_NOPE,), g.dtype), g[MLA_NOPE + half:], g[MLA_NOPE:MLA_NOPE + half], zpad])
    return jnp.stack([plain, sw])


def _rope_tiles(seq):
    pos = jnp.arange(seq, dtype=f32)
    inv_freq = ROPE_BASE ** (-jnp.arange(0, MLA_ROPE, 2, dtype=f32) / MLA_ROPE)
    ang = pos[:, None] * inv_freq[None, :]
    cos, sin = jnp.cos(ang), jnp.sin(ang)
    zpad = jnp.zeros((seq, HEAD_PAD - MLA_QK), f32)
    cos_t = jnp.concatenate([jnp.ones((seq, MLA_NOPE), f32), cos, cos, zpad], axis=1)
    sin_t = jnp.concatenate([jnp.zeros((seq, MLA_NOPE), f32), -sin, sin, zpad], axis=1)
    return cos_t, sin_t


def _block_diag_const(n, blk, lower):
    r = jnp.arange(n)[:, None]
    c = jnp.arange(n)[None, :]
    m = (r // blk) == (c // blk)
    if lower:
        m = m & (c <= r)
    return m.astype(bf16)


@jax.jit
def kernel(x, ffn1_norm, ffn1_w_gate, ffn1_w_up, ffn1_w_down, mix_norm, w_in, q_lat_norm, w_q_up, kv_lat_norm, w_kv_up, q_norm, k_norm, shift_mu, w0, w2, a0, a2, g2, k_k, k_a, r_k, gn_gain, gn_bias, out_norm_mla, out_norm_rwkv, w_out, ffn2_norm, ffn2_w_gate, ffn2_w_up, ffn2_w_down):
    batch, seq, d = x.shape
    xt = x.reshape(batch * seq, d)
    cos_t, sin_t = _rope_tiles(seq)
    e64 = _block_diag_const(RWKV_W, RWKV_HEAD, lower=False)
    tri = _block_diag_const(TRI_ROWS, CHUNK, lower=True)
    row = lambda a: a[None, :]
    wg1, wu1, wd1 = ffn1_w_gate.astype(bf16), ffn1_w_up.astype(bf16), ffn1_w_down.astype(bf16)
    wg2, wu2, wd2 = ffn2_w_gate.astype(bf16), ffn2_w_up.astype(bf16), ffn2_w_down.astype(bf16)
    win = _win_ext(w_in.astype(bf16))
    wo = w_out.astype(bf16)
    for l in range(DEPTH):
        xt, pm, pr = _ffn_in(xt, row(ffn1_norm[l]), wg1, wu1, wd1, row(mix_norm[l]), win, l)
        wq = jnp.concatenate([_head_pad_cols(w_q_up[l], MLA_QK), _swap_rope_cols(w_q_up[l])], axis=1).astype(bf16)
        wkv3 = w_kv_up[l].reshape(KV_LORA, MLA_HEADS, MLA_NOPE + MLA_V)
        wk = _head_pad_cols(wkv3[..., :MLA_NOPE].reshape(KV_LORA, -1), MLA_NOPE).astype(bf16)
        wvt = wkv3[..., MLA_NOPE:].reshape(KV_LORA, MLA_W).T.astype(bf16)
        q, k, vt = _mla_prep(pm, cos_t, sin_t, row(q_lat_norm[l]), row(kv_lat_norm[l]), wq, wk, wvt,
                             _norm_pair(q_norm[l]), _norm_pair(k_norm[l]), seq)
        o = _attention(q, k, vt, batch, seq)
        zl = jnp.zeros((DECAY_LORA, RWKV_W), f32)
        wwa = jnp.concatenate([jnp.concatenate([w2[l], zl], axis=1), jnp.concatenate([zl, a2[l]], axis=1)], axis=0)
        g2p = jnp.pad(g2[l], ((0, GATE_PAD - GATE_LORA), (0, 0)))
        y = _rwkv(pr, _mu_ext(shift_mu[l]), row(w0[l]), row(a0[l]), wwa.astype(bf16), g2p.astype(bf16),
                  row(k_k[l]), row(k_a[l]), row(r_k[l].reshape(-1)), row(gn_gain[l]), row(gn_bias[l]),
                  e64, tri, batch, seq)
        xt = _out_ffn(xt, o, y, row(out_norm_mla[l]), row(out_norm_rwkv[l]), wo, row(ffn2_norm[l]), wg2, wu2, wd2, l)
    return xt.reshape(batch, seq, d)
```

```python
import jax
import jax.numpy as jnp
from jax import lax
from jax.experimental import pallas as pl
from jax.experimental.pallas import tpu as pltpu

f32 = jnp.float32
bf16 = jnp.bfloat16

D_MODEL = 1024
DEPTH = 4
CHUNK = 64
D_FF = 2816
EPS = 1e-6
MLA_HEADS = 8
MLA_NOPE = 64
MLA_ROPE = 32
MLA_QK = MLA_NOPE + MLA_ROPE
MLA_V = 64
Q_LORA = 256
KV_LORA = 128
ROPE_BASE = 10000.0
RWKV_HEADS = 8
RWKV_HEAD = 64
RWKV_W = RWKV_HEADS * RWKV_HEAD
DECAY_LORA = 64
A_LORA = 64
GATE_LORA = 160
GN_EPS = 64e-5
MLA_W = MLA_HEADS * MLA_V
MLA_IN = Q_LORA + KV_LORA + MLA_ROPE

LANES = 128
HEAD_PAD = LANES
QK_PAD_W = MLA_HEADS * HEAD_PAD
P_MLA_W = Q_LORA + KV_LORA + 2 * LANES
GATE_PAD = 2 * LANES
P_RWKV_W = 3 * RWKV_W + LANES + GATE_PAD
VMEM_LIMIT = 56 * 1024 * 1024

TM_FFN = 256
TM_OUT_FFN = 512
TM_PREP = 512
TQ = 512
HALF_TQ = TQ // 2
ONES_ROWS = 16
LOG2_E = 1.4426950408889634
TM_RWKV = 1024
TRI_ROWS = 256
N_PAIR = RWKV_HEADS // 2


def _rms(x, g):
    return x * lax.rsqrt(jnp.mean(x * x, axis=-1, keepdims=True) + EPS) * g


def _dot(a, b):
    return jnp.dot(a, b, preferred_element_type=f32)


def _dot_nt(a, b):
    return lax.dot_general(a, b, (((1,), (1,)), ((), ())), preferred_element_type=f32)


def _dot_tn(a, b):
    return lax.dot_general(a, b, (((0,), (0,)), ((), ())), preferred_element_type=f32)


def _dot_split(c, x):
    hi = x.astype(bf16)
    lo = (x - hi.astype(f32)).astype(bf16)
    return _dot(c, hi) + _dot(c, lo)


def _swiglu_half(x, g, wg_ref, wu_ref, wd_ref):
    h = _rms(x, g).astype(bf16)
    gate = _dot(h, wg_ref[...])
    up = _dot(h, wu_ref[...])
    act = (jax.nn.silu(gate) * up).astype(bf16)
    return x + 0.5 * _dot(act, wd_ref[...])


def _ffn_in_kernel(x_ref, g1_ref, wg_ref, wu_ref, wd_ref, gm_ref, win_ref, x1_ref, pm_ref, pr_ref):
    x1 = _swiglu_half(x_ref[...], g1_ref[...], wg_ref, wu_ref, wd_ref)
    x1_ref[...] = x1
    p = _dot(_rms(x1, gm_ref[...]).astype(bf16), win_ref[...])
    pm_ref[...] = p[:, :P_MLA_W]
    pr_ref[...] = p[:, P_MLA_W:]


def _out_ffn_kernel(x_ref, o_ref, y_ref, go_ref, gy_ref, wo_ref, g2_ref, wg_ref, wu_ref, wd_ref, x3_ref):
    ho = _rms(o_ref[...], go_ref[...]).astype(bf16)
    hy = _rms(y_ref[...], gy_ref[...]).astype(bf16)
    x2 = x_ref[...] + _dot(ho, wo_ref[:MLA_W, :]) + _dot(hy, wo_ref[MLA_W:, :])
    x3_ref[...] = _swiglu_half(x2, g2_ref[...], wg_ref, wu_ref, wd_ref)


def _mla_prep_kernel(pm_ref, cos_ref, sin_ref, gq_ref, gkv_ref, wq_ref, wk_ref, wvt_ref, qn_ref, kn_ref,
                     q_ref, k_ref, vt_ref):
    pm = pm_ref[...]
    q_lat = pm[:, :Q_LORA]
    kv_lat = pm[:, Q_LORA:Q_LORA + KV_LORA]
    kpe = pm[:, Q_LORA + KV_LORA:Q_LORA + KV_LORA + LANES]
    kpe_sw = pm[:, Q_LORA + KV_LORA + LANES:]
    cos_t = cos_ref[...]
    sin_t = sin_ref[...]
    q2 = _dot(_rms(q_lat, gq_ref[...]).astype(bf16), wq_ref[...])
    hkv = _rms(kv_lat, gkv_ref[...]).astype(bf16)
    kv = _dot(hkv, wk_ref[...])
    vt_ref[...] = _dot_nt(wvt_ref[...], hkv).astype(bf16)
    qc = qn_ref[0:1, :] * cos_t
    qs = qn_ref[1:2, :] * sin_t
    kc = kn_ref[0:1, :] * cos_t
    ks = kn_ref[1:2, :] * sin_t
    scale = MLA_QK ** -0.5 * LOG2_E
    for h in range(MLA_HEADS):
        sl = slice(h * HEAD_PAD, (h + 1) * HEAD_PAD)
        qh = q2[:, sl]
        qsw = q2[:, QK_PAD_W + h * HEAD_PAD:QK_PAD_W + (h + 1) * HEAD_PAD]
        rs = lax.rsqrt(jnp.sum(qh * qh, axis=-1, keepdims=True) * (1.0 / MLA_QK) + EPS)
        q_ref[:, sl] = ((qh * qc + qsw * qs) * (rs * scale)).astype(bf16)
        kh = kv[:, sl] + kpe
        rk = lax.rsqrt(jnp.sum(kh * kh, axis=-1, keepdims=True) * (1.0 / MLA_QK) + EPS)
        k_ref[:, sl] = ((kh * kc + kpe_sw * ks) * rk).astype(bf16)


def _attn_kernel(q_ref, k_ref, vt_ref, o_ref, s0_scr, s1_scr):
    n_tiles = q_ref.shape[0] // TQ
    chunk_gap = (lax.broadcasted_iota(jnp.int32, (HALF_TQ, HALF_TQ), 0) // CHUNK
                 - lax.broadcasted_iota(jnp.int32, (HALF_TQ, HALF_TQ), 1) // CHUNK)
    diag_visible = chunk_gap <= 0
    heads = [slice(h * HEAD_PAD, (h + 1) * HEAD_PAD) for h in range(2)]
    ones_rows = jnp.ones((ONES_ROWS, TQ), bf16)
    bufs = (s0_scr, s1_scr)
    tile = lambda n: slice(n * TQ, (n + 1) * TQ)
    lo_half = lambda n: slice(n * TQ, n * TQ + HALF_TQ)
    hi_half = lambda n: slice(n * TQ + HALF_TQ, (n + 1) * TQ)

    def scores(step, s_scr):
        i, j = step
        col_max = []
        for h in range(2):
            if j == i:
                top = _dot_nt(k_ref[lo_half(j), heads[h]], q_ref[tile(i), heads[h]])
                top = jnp.concatenate([jnp.where(diag_visible, top[:, :HALF_TQ], -jnp.inf), top[:, HALF_TQ:]], axis=1)
                bot = _dot_nt(k_ref[hi_half(j), heads[h]], q_ref[hi_half(i), heads[h]])
                bot = jnp.where(diag_visible, bot, -jnp.inf)
                s_scr[h, :HALF_TQ, :] = top
                s_scr[h, HALF_TQ:, HALF_TQ:] = bot
                top_max = jnp.max(top, axis=0, keepdims=True)
                col_max.append(jnp.concatenate(
                    [top_max[:, :HALF_TQ], jnp.maximum(top_max[:, HALF_TQ:], jnp.max(bot, axis=0, keepdims=True))],
                    axis=1))
            else:
                st = _dot_nt(k_ref[tile(j), heads[h]], q_ref[tile(i), heads[h]])
                s_scr[h] = st
                col_max.append(jnp.max(st, axis=0, keepdims=True))
        return col_max

    def vt_aug(h, cols):
        return jnp.concatenate([vt_ref[h * MLA_V:(h + 1) * MLA_V, cols], ones_rows[:, :cols.stop - cols.start]], axis=0)

    def consume(step, s_scr, col_max, stats):
        i, j = step
        new = []
        for h in range(2):
            m, acc = stats[h]
            m_new = jnp.maximum(m, col_max[h])
            alpha = jnp.exp2(m - m_new)
            if j == i:
                p_top = jnp.exp2(s_scr[h, :HALF_TQ, :] - m_new).astype(bf16)
                p_bot = jnp.exp2(s_scr[h, HALF_TQ:, HALF_TQ:] - m_new[:, HALF_TQ:]).astype(bf16)
                upd = _dot(vt_aug(h, lo_half(j)), p_top)
                upd_hi = _dot(vt_aug(h, hi_half(j)), p_bot)
                upd = jnp.concatenate([upd[:, :HALF_TQ], upd[:, HALF_TQ:] + upd_hi], axis=1)
            else:
                upd = _dot(vt_aug(h, tile(j)), jnp.exp2(s_scr[h] - m_new).astype(bf16))
            new.append((m_new, alpha * acc + upd))
        return tuple(new)

    steps = [(i, j) for i in range(n_tiles) for j in range(i + 1)]
    next_max = scores(steps[0], bufs[0])
    stats = None
    for n, step in enumerate(steps):
        i, j = step
        cur_max = next_max
        if n + 1 < len(steps):
            next_max = scores(steps[n + 1], bufs[(n + 1) % 2])
        if j == 0:
            stats = tuple((jnp.full((1, TQ), -jnp.inf, f32), jnp.zeros((MLA_V + ONES_ROWS, TQ), f32))
                          for _ in range(2))
        stats = consume(step, bufs[n % 2], cur_max, stats)
        if j == i:
            (_, acc0), (_, acc1) = stats
            o_ref[tile(i), :] = jnp.concatenate([acc0[:MLA_V] / acc0[MLA_V:MLA_V + 1],
                                                 acc1[:MLA_V] / acc1[MLA_V:MLA_V + 1]], axis=0).T


def _rwkv_kernel(p_ref, mu_ref, w0_ref, a0_ref, wwa_ref, g2_ref, kk_ref, ka_ref, rk_ref, gng_ref, gnb_ref,
                 e64_ref, tri_ref, y_ref,
                 shift_scr, state_scr, at_scr, rt_scr, bt_scr, kt_scr, bp_scr, kp_scr, v_scr, pc_scr):
    tm = TM_RWKV
    i = pl.program_id(1)

    @pl.when(i == 0)
    def _():
        shift_scr[0:8, :] = jnp.zeros((8, P_RWKV_W), f32)
        state_scr[...] = jnp.zeros_like(state_scr)

    cur = p_ref[...]
    shift_scr[8:8 + tm, :] = cur
    prev = shift_scr[7:7 + tm, :]
    xs = cur + mu_ref[...] * (prev - cur)
    shift_scr[0:8, :] = shift_scr[tm:tm + 8, :]

    r = xs[:, 0:RWKV_W]
    k = xs[:, RWKV_W:2 * RWKV_W]
    v = xs[:, 2 * RWKV_W:3 * RWKV_W]
    wa = xs[:, 3 * RWKV_W:3 * RWKV_W + LANES]
    gl = xs[:, 3 * RWKV_W + LANES:]

    lane = lax.broadcasted_iota(jnp.int32, (tm, LANES), 1)
    wa = jnp.where(lane < DECAY_LORA, jnp.tanh(wa), wa)
    wa_pre = _dot(wa.astype(bf16), wwa_ref[...])
    z = -(w0_ref[...] + wa_pre[:, :RWKV_W])
    softplus = jnp.maximum(z, 0.0) + jnp.log(1.0 + jnp.exp(-jnp.abs(z)))
    logd = -jnp.exp(-softplus - 0.5)
    a = jax.nn.sigmoid(a0_ref[...] + wa_pre[:, RWKV_W:])
    g = _dot(jax.nn.sigmoid(gl).astype(bf16), g2_ref[...])

    e64 = e64_ref[...]
    kk = k * kk_ref[...]
    kk = kk / jnp.maximum(jnp.sqrt(_dot((kk * kk).astype(bf16), e64)), 1e-12)
    kmod = k * (1.0 + (a - 1.0) * ka_ref[...])
    bonus = _dot((r * kmod * rk_ref[...]).astype(bf16), e64) * v

    tri_rows = tri_ref.shape[0]
    cl = jnp.concatenate([_dot_split(tri_ref[...], logd[s * tri_rows:(s + 1) * tri_rows])
                          for s in range(tm // tri_rows)], axis=0)
    cl3 = cl.reshape(tm // CHUNK, CHUNK, RWKV_W)
    cl_end = jnp.broadcast_to(cl3[:, CHUNK - 1:CHUNK, :], cl3.shape).reshape(tm, RWKV_W)
    at_scr[...] = (-kk * jnp.exp(cl - logd)).astype(bf16)
    rt_scr[...] = (r * jnp.exp(cl)).astype(bf16)
    pinv = jnp.exp(-cl)
    bt_scr[...] = (kk * a * pinv).astype(bf16)
    kt_scr[...] = (kmod * pinv).astype(bf16)
    chunk_decay = jnp.exp(cl_end)
    q_end = chunk_decay * pinv
    bp_scr[...] = (kk * a * q_end).astype(bf16)
    kp_scr[...] = (kmod * q_end).astype(bf16)
    v_scr[...] = v.astype(bf16)
    pc_scr[...] = chunk_decay

    c2 = 2 * CHUNK
    lane_c = lax.broadcasted_iota(jnp.int32, (CHUNK, c2), 1)
    row_c = lax.broadcasted_iota(jnp.int32, (CHUNK, c2), 0)
    left = lane_c < CHUNK
    s_idx = jnp.where(left, lane_c, lane_c - CHUNK)
    strict = s_idx < row_c
    incl = s_idx <= row_c
    eye_sbs = (s_idx == row_c).astype(f32)
    level_masks = [((row_c >> (lvl + 1)) == (s_idx >> (lvl + 1))) & (((row_c >> lvl) & 1) == 1)
                   & (((s_idx >> lvl) & 1) == 0) for lvl in range(6)]
    bd_mask = ((lax.broadcasted_iota(jnp.int32, (c2, c2), 0) // CHUNK)
               == (lax.broadcasted_iota(jnp.int32, (c2, c2), 1) // CHUNK))

    def stack2(x):
        zero = jnp.zeros_like(x)
        return jnp.concatenate([jnp.where(left, x, zero), jnp.where(left, zero, x)], axis=0)

    n_chunk = tm // CHUNK
    units = [(c, p) for c in range(n_chunk) for p in range(N_PAIR)]

    def region(c, p):
        return slice(c * CHUNK, (c + 1) * CHUNK), slice(p * LANES, (p + 1) * LANES)

    ar, vb, vst, a_ab, a_ak, a_r, t_inv, w1 = {}, {}, {}, {}, {}, {}, {}, {}
    for u_ in units:
        rows, cols = region(*u_)
        ar[u_] = jnp.concatenate([at_scr[rows, cols], rt_scr[rows, cols]], axis=0)
        yk = jnp.concatenate([stack2(bt_scr[rows, cols]), stack2(kt_scr[rows, cols])], axis=0)
        gm = _dot_nt(ar[u_], yk)
        a_ab[u_] = gm[:CHUNK, :c2]
        a_ak[u_] = jnp.where(strict, gm[:CHUNK, c2:], 0.0).astype(bf16)
        a_r[u_] = jnp.concatenate([jnp.where(incl, gm[CHUNK:, :c2], 0.0),
                                   jnp.where(incl, gm[CHUNK:, c2:], 0.0)], axis=1).astype(bf16)
        t_inv[u_] = eye_sbs + jnp.where(level_masks[0], a_ab[u_], 0.0)
    for lvl in range(1, 6):
        nt = {}
        for u_ in units:
            n_k = jnp.where(level_masks[lvl], a_ab[u_], 0.0).astype(bf16)
            nt[u_] = _dot(n_k, stack2(t_inv[u_].astype(bf16)))
        for u_ in units:
            t_inv[u_] = t_inv[u_] + _dot(t_inv[u_].astype(bf16), stack2(nt[u_].astype(bf16)))
    for u_ in units:
        rows, cols = region(*u_)
        vb[u_] = v_scr[rows, cols]
        vst[u_] = stack2(vb[u_])
        w1[u_] = _dot(a_ak[u_], vst[u_])
    a2, u1 = {}, {}
    for u_ in units:
        au = _dot(t_inv[u_].astype(bf16),
                  jnp.concatenate([stack2(ar[u_][:CHUNK]), stack2(w1[u_].astype(bf16))], axis=1))
        a2[u_] = au[:, :c2].astype(bf16)
        u1[u_] = au[:, c2:].astype(bf16)
    m_c, n_c, rt2, y1 = {}, {}, {}, {}
    for u_ in units:
        rows, cols = region(*u_)
        lhs = jnp.concatenate([jnp.concatenate([a2[u_], u1[u_]], axis=1),
                               jnp.concatenate([jnp.zeros_like(vb[u_]), vb[u_]], axis=1)], axis=0)
        mn = _dot_tn(lhs, jnp.concatenate([bp_scr[rows, cols], kp_scr[rows, cols]], axis=0))
        m_c[u_] = jnp.where(bd_mask, mn[:c2], 0.0).astype(bf16)
        n_c[u_] = jnp.where(bd_mask, mn[c2:], 0.0)
    for u_ in units:
        rt2[u_] = (ar[u_][CHUNK:].astype(f32) + _dot(a_r[u_][:, :c2], stack2(a2[u_]))).astype(bf16)
    for u_ in units:
        y1[u_] = _dot(a_r[u_], jnp.concatenate([stack2(u1[u_]), vst[u_]], axis=0))

    ys = [[None] * N_PAIR for _ in range(n_chunk)]
    ht = [state_scr[p] for p in range(N_PAIR)]
    for c in range(n_chunk):
        hb = [ht[p].astype(bf16) for p in range(N_PAIR)]
        for p in range(N_PAIR):
            cols = region(c, p)[1]
            ht[p] = ht[p] * pc_scr[c * CHUNK:c * CHUNK + 1, cols] + _dot(hb[p], m_c[c, p]) + n_c[c, p]
        for p in range(N_PAIR):
            ys[c][p] = _dot_nt(rt2[c, p], hb[p]) + y1[c, p]
    for p in range(N_PAIR):
        state_scr[p] = ht[p]

    y = jnp.concatenate([jnp.concatenate(row_y, axis=1) for row_y in ys], axis=0)
    inv_n = 1.0 / RWKV_HEAD
    mean = _dot(y.astype(bf16), e64) * inv_n
    yc = y - mean
    var = _dot((yc * yc).astype(bf16), e64) * inv_n
    yn = yc * lax.rsqrt(var + GN_EPS) * gng_ref[...] + gnb_ref[...]
    y_ref[...] = (yn + bonus) * g


def _const_spec(shape):
    return pl.BlockSpec(shape, lambda *_: (0,) * len(shape), pipeline_mode=pl.Buffered(1))


def _layer_spec(stacked, layer):
    rest = stacked.shape[1:]
    return pl.BlockSpec((None,) + rest, lambda *_: (layer,) + (0,) * len(rest), pipeline_mode=pl.Buffered(1))


def _params(sem):
    return pltpu.CompilerParams(dimension_semantics=sem, vmem_limit_bytes=VMEM_LIMIT)


def _ffn_in(x, g1, wg, wu, wd, gm, win, layer):
    t = x.shape[0]
    tm = TM_FFN
    row = lambda w: pl.BlockSpec((tm, w), lambda i: (i, 0))
    return pl.pallas_call(
        _ffn_in_kernel,
        grid=(t // tm,),
        in_specs=[row(D_MODEL), _const_spec(g1.shape), _layer_spec(wg, layer), _layer_spec(wu, layer),
                  _layer_spec(wd, layer), _const_spec(gm.shape), _layer_spec(win, layer)],
        out_specs=[row(D_MODEL), row(P_MLA_W), row(P_RWKV_W)],
        out_shape=[jax.ShapeDtypeStruct((t, D_MODEL), f32), jax.ShapeDtypeStruct((t, P_MLA_W), f32),
                   jax.ShapeDtypeStruct((t, P_RWKV_W), f32)],
        compiler_params=_params(("parallel",)),
        name="ffn_in",
    )(x, g1, wg, wu, wd, gm, win)


def _out_ffn(x1, o, y, go, gy, wo, g2, wg, wu, wd, layer):
    t = x1.shape[0]
    tm = TM_OUT_FFN
    row = lambda w: pl.BlockSpec((tm, w), lambda i: (i, 0))
    return pl.pallas_call(
        _out_ffn_kernel,
        grid=(t // tm,),
        in_specs=[row(D_MODEL), row(MLA_W), row(RWKV_W), _const_spec(go.shape), _const_spec(gy.shape),
                  _layer_spec(wo, layer), _const_spec(g2.shape), _layer_spec(wg, layer), _layer_spec(wu, layer),
                  _layer_spec(wd, layer)],
        out_specs=row(D_MODEL),
        out_shape=jax.ShapeDtypeStruct((t, D_MODEL), f32),
        compiler_params=_params(("parallel",)),
        name="out_ffn",
    )(x1, o, y, go, gy, wo, g2, wg, wu, wd)


def _mla_prep(pm, cos_t, sin_t, gq, gkv, wq, wk, wvt, qn, kn, seq):
    t = pm.shape[0]
    tm = TM_PREP
    n_pos = seq // tm
    row = lambda w: pl.BlockSpec((tm, w), lambda i: (i, 0))
    pos = pl.BlockSpec((tm, LANES), lambda i: (i % n_pos, 0))
    return pl.pallas_call(
        _mla_prep_kernel,
        grid=(t // tm,),
        in_specs=[row(P_MLA_W), pos, pos, _const_spec(gq.shape), _const_spec(gkv.shape), _const_spec(wq.shape),
                  _const_spec(wk.shape), _const_spec(wvt.shape), _const_spec(qn.shape), _const_spec(kn.shape)],
        out_specs=[row(QK_PAD_W), row(QK_PAD_W), pl.BlockSpec((MLA_W, tm), lambda i: (0, i))],
        out_shape=[jax.ShapeDtypeStruct((t, QK_PAD_W), bf16), jax.ShapeDtypeStruct((t, QK_PAD_W), bf16),
                   jax.ShapeDtypeStruct((MLA_W, t), bf16)],
        compiler_params=_params(("parallel",)),
        name="mla_prep",
    )(pm, cos_t, sin_t, gq, gkv, wq, wk, wvt, qn, kn)


def _attention(q, k, vt, batch, seq):
    return pl.pallas_call(
        _attn_kernel,
        grid=(batch, MLA_HEADS // 2),
        in_specs=[pl.BlockSpec((seq, 2 * HEAD_PAD), lambda b, hp: (b, hp)),
                  pl.BlockSpec((seq, 2 * HEAD_PAD), lambda b, hp: (b, hp)),
                  pl.BlockSpec((2 * MLA_V, seq), lambda b, hp: (hp, b))],
        out_specs=pl.BlockSpec((seq, 2 * MLA_V), lambda b, hp: (b, hp)),
        out_shape=jax.ShapeDtypeStruct((batch * seq, MLA_W), f32),
        scratch_shapes=[pltpu.VMEM((2, TQ, TQ), f32)] * 2,
        compiler_params=_params(("parallel", "parallel")),
        name="attention",
    )(q, k, vt)


def _rwkv(pr, mu, w0, a0, wwa, g2, kk, ka, rk, gng, gnb, e64, tri, batch, seq):
    tm = TM_RWKV
    ns = seq // tm
    vecs = [mu, w0, a0, wwa, g2, kk, ka, rk, gng, gnb, e64, tri]
    return pl.pallas_call(
        _rwkv_kernel,
        grid=(batch, ns),
        in_specs=[pl.BlockSpec((tm, P_RWKV_W), lambda b, i: (b * ns + i, 0))] + [_const_spec(a.shape) for a in vecs],
        out_specs=pl.BlockSpec((tm, RWKV_W), lambda b, i: (b * ns + i, 0)),
        out_shape=jax.ShapeDtypeStruct((batch * seq, RWKV_W), f32),
        scratch_shapes=[pltpu.VMEM((tm + 8, P_RWKV_W), f32),
                        pltpu.VMEM((N_PAIR, LANES, LANES), f32)]
                       + [pltpu.VMEM((tm, RWKV_W), bf16)] * 7
                       + [pltpu.VMEM((tm, RWKV_W), f32)],
        compiler_params=_params(("arbitrary", "arbitrary")),
        name="rwkv",
    )(pr, *vecs)


def _win_ext(w_in):
    z = lambda n: jnp.zeros(w_in.shape[:-1] + (n,), w_in.dtype)
    pe = w_in[..., Q_LORA + KV_LORA:MLA_IN]
    half = MLA_ROPE // 2
    rw = w_in[..., MLA_IN:]
    o = 0
    r_w = rw[..., o:o + RWKV_W]; o += RWKV_W
    wl_w = rw[..., o:o + DECAY_LORA]; o += DECAY_LORA
    k_w = rw[..., o:o + RWKV_W]; o += RWKV_W
    v_w = rw[..., o:o + RWKV_W]; o += RWKV_W
    al_w = rw[..., o:o + A_LORA]; o += A_LORA
    gl_w = rw[..., o:o + GATE_LORA]
    return jnp.concatenate([
        w_in[..., :Q_LORA + KV_LORA],
        z(MLA_NOPE), pe, z(LANES - MLA_QK),
        z(MLA_NOPE), pe[..., half:], pe[..., :half], z(LANES - MLA_QK),
        r_w, k_w, v_w, wl_w, al_w, gl_w, z(GATE_PAD - GATE_LORA)], axis=-1)


def _mu_ext(mu):
    o = 0
    r_m = mu[o:o + RWKV_W]; o += RWKV_W
    wl_m = mu[o:o + DECAY_LORA]; o += DECAY_LORA
    k_m = mu[o:o + RWKV_W]; o += RWKV_W
    v_m = mu[o:o + RWKV_W]; o += RWKV_W
    al_m = mu[o:o + A_LORA]; o += A_LORA
    gl_m = mu[o:o + GATE_LORA]
    return jnp.concatenate([r_m, k_m, v_m, wl_m, al_m, gl_m, jnp.zeros((GATE_PAD - GATE_LORA,), mu.dtype)])[None, :]


def _head_pad_cols(w, width):
    kdim = w.shape[0]
    w = w.reshape(kdim, MLA_HEADS, width)
    return jnp.pad(w, ((0, 0), (0, 0), (0, HEAD_PAD - width))).reshape(kdim, QK_PAD_W)


def _swap_rope_cols(w):
    kdim = w.shape[0]
    half = MLA_ROPE // 2
    w = w.reshape(kdim, MLA_HEADS, MLA_QK)
    sw = jnp.concatenate([jnp.zeros((kdim, MLA_HEADS, MLA_NOPE), w.dtype), w[..., MLA_NOPE + half:],
                          w[..., MLA_NOPE:MLA_NOPE + half],
                          jnp.zeros((kdim, MLA_HEADS, HEAD_PAD - MLA_QK), w.dtype)], axis=-1)
    return sw.reshape(kdim, QK_PAD_W)


def _norm_pair(g):
    half = MLA_ROPE // 2
    zpad = jnp.zeros((HEAD_PAD - MLA_QK,), g.dtype)
    plain = jnp.concatenate([g, zpad])
    sw = jnp.concatenate([jnp.zeros((MLA_NOPE,), g.dtype), g[MLA_NOPE + half:], g[MLA_NOPE:MLA_NOPE + half], zpad])
    return jnp.stack([plain, sw])


def _rope_tiles(seq):
    pos = jnp.arange(seq, dtype=f32)
    inv_freq = ROPE_BASE ** (-jnp.arange(0, MLA_ROPE, 2, dtype=f32) / MLA_ROPE)
    ang = pos[:, None] * inv_freq[None, :]
    cos, sin = jnp.cos(ang), jnp.sin(ang)
    zpad = jnp.zeros((seq, HEAD_PAD - MLA_QK), f32)
    cos_t = jnp.concatenate([jnp.ones((seq, MLA_NOPE), f32), cos, cos, zpad], axis=1)
    sin_t = jnp.concatenate([jnp.zeros((seq, MLA_NOPE), f32), -sin, sin, zpad], axis=1)
    return cos_t, sin_t


def _block_diag_const(n, blk, lower):
    r = jnp.arange(n)[:, None]
    c = jnp.arange(n)[None, :]
    m = (r // blk) == (c // blk)
    if lower:
        m = m & (c <= r)
    return m.astype(bf16)


@jax.jit
def kernel(x, ffn1_norm, ffn1_w_gate, ffn1_w_up, ffn1_w_down, mix_norm, w_in, q_lat_norm, w_q_up, kv_lat_norm, w_kv_up, q_norm, k_norm, shift_mu, w0, w2, a0, a2, g2, k_k, k_a, r_k, gn_gain, gn_bias, out_norm_mla, out_norm_rwkv, w_out, ffn2_norm, ffn2_w_gate, ffn2_w_up, ffn2_w_down):
    batch, seq, d = x.shape
    xt = x.reshape(batch * seq, d)
    cos_t, sin_t = _rope_tiles(seq)
    e64 = _block_diag_const(RWKV_W, RWKV_HEAD, lower=False)
    tri = _block_diag_const(TRI_ROWS, CHUNK, lower=True)
    row = lambda a: a[None, :]
    wg1, wu1, wd1 = ffn1_w_gate.astype(bf16), ffn1_w_up.astype(bf16), ffn1_w_down.astype(bf16)
    wg2, wu2, wd2 = ffn2_w_gate.astype(bf16), ffn2_w_up.astype(bf16), ffn2_w_down.astype(bf16)
    win = _win_ext(w_in.astype(bf16))
    wo = w_out.astype(bf16)
    for l in range(DEPTH):
        xt, pm, pr = _ffn_in(xt, row(ffn1_norm[l]), wg1, wu1, wd1, row(mix_norm[l]), win, l)
        wq = jnp.concatenate([_head_pad_cols(w_q_up[l], MLA_QK), _swap_rope_cols(w_q_up[l])], axis=1).astype(bf16)
        wkv3 = w_kv_up[l].reshape(KV_LORA, MLA_HEADS, MLA_NOPE + MLA_V)
        wk = _head_pad_cols(wkv3[..., :MLA_NOPE].reshape(KV_LORA, -1), MLA_NOPE).astype(bf16)
        wvt = wkv3[..., MLA_NOPE:].reshape(KV_LORA, MLA_W).T.astype(bf16)
        q, k, vt = _mla_prep(pm, cos_t, sin_t, row(q_lat_norm[l]), row(kv_lat_norm[l]), wq, wk, wvt,
                             _norm_pair(q_norm[l]), _norm_pair(k_norm[l]), seq)
        o = _attention(q, k, vt, batch, seq)
        zl = jnp.zeros((DECAY_LORA, RWKV_W), f32)
        wwa = jnp.concatenate([jnp.concatenate([w2[l], zl], axis=1), jnp.concatenate([zl, a2[l]], axis=1)], axis=0)
        g2p = jnp.pad(g2[l], ((0, GATE_PAD - GATE_LORA), (0, 0)))
        y = _rwkv(pr, _mu_ext(shift_mu[l]), row(w0[l]), row(a0[l]), wwa.astype(bf16), g2p.astype(bf16),
                  row(k_k[l]), row(k_a[l]), row(r_k[l].reshape(-1)), row(gn_gain[l]), row(gn_bias[l]),
                  e64, tri, batch, seq)
        xt = _out_ffn(xt, o, y, row(out_norm_mla[l]), row(out_norm_rwkv[l]), wo, row(ffn2_norm[l]), wg2, wu2, wd2, l)
    return xt.reshape(batch, seq, d)
```

```python
import jax
import jax.numpy as jnp
from jax import lax
from jax.experimental import pallas as pl
from jax.experimental.pallas import tpu as pltpu

f32 = jnp.float32
bf16 = jnp.bfloat16

D_MODEL = 1024
DEPTH = 4
CHUNK = 64
D_FF = 2816
EPS = 1e-6
MLA_HEADS = 8
MLA_NOPE = 64
MLA_ROPE = 32
MLA_QK = MLA_NOPE + MLA_ROPE
MLA_V = 64
Q_LORA = 256
KV_LORA = 128
ROPE_BASE = 10000.0
RWKV_HEADS = 8
RWKV_HEAD = 64
RWKV_W = RWKV_HEADS * RWKV_HEAD
DECAY_LORA = 64
A_LORA = 64
GATE_LORA = 160
GN_EPS = 64e-5
MLA_W = MLA_HEADS * MLA_V
MLA_IN = Q_LORA + KV_LORA + MLA_ROPE

LANES = 128
HEAD_PAD = LANES
QK_PAD_W = MLA_HEADS * HEAD_PAD
P_MLA_W = Q_LORA + KV_LORA + 2 * LANES
GATE_PAD = 2 * LANES
P_RWKV_W = 3 * RWKV_W + LANES + GATE_PAD
VMEM_LIMIT = 56 * 1024 * 1024

TM_FFN = 512
FF_SLAB = 1536
TM_OUT_FFN = 512
TM_PREP = 512
TQ = 512
HALF_TQ = TQ // 2
ONES_ROWS = 16
LOG2_E = 1.4426950408889634
TM_RWKV = 1024
TRI_ROWS = 256
N_PAIR = RWKV_HEADS // 2


def _rms(x, g):
    return x * lax.rsqrt(jnp.mean(x * x, axis=-1, keepdims=True) + EPS) * g


def _dot(a, b):
    return jnp.dot(a, b, preferred_element_type=f32)


def _dot_nt(a, b):
    return lax.dot_general(a, b, (((1,), (1,)), ((), ())), preferred_element_type=f32)


def _dot_tn(a, b):
    return lax.dot_general(a, b, (((0,), (0,)), ((), ())), preferred_element_type=f32)


def _dot_split(c, x):
    hi = x.astype(bf16)
    lo = (x - hi.astype(f32)).astype(bf16)
    return _dot(c, hi) + _dot(c, lo)


def _swiglu_half(x, g, wg_ref, wu_ref, wd_ref):
    h = _rms(x, g).astype(bf16)
    gate = _dot(h, wg_ref[...])
    up = _dot(h, wu_ref[...])
    act = (jax.nn.silu(gate) * up).astype(bf16)
    return x + 0.5 * _dot(act, wd_ref[...])


def _ffn_in_kernel(x_ref, g1_ref, wg_ref, wu_ref, wd_ref, gm_ref, win_ref, x1_ref, pm_ref, pr_ref):
    x = x_ref[...]
    h = _rms(x, g1_ref[...]).astype(bf16)
    down = None
    for cols in (slice(0, FF_SLAB), slice(FF_SLAB, D_FF)):
        act = (jax.nn.silu(_dot(h, wg_ref[:, cols])) * _dot(h, wu_ref[:, cols])).astype(bf16)
        part = _dot(act, wd_ref[cols, :])
        down = part if down is None else down + part
    x1 = x + 0.5 * down
    x1_ref[...] = x1
    p = _dot(_rms(x1, gm_ref[...]).astype(bf16), win_ref[...])
    pm_ref[...] = p[:, :P_MLA_W]
    pr_ref[...] = p[:, P_MLA_W:]


def _out_ffn_kernel(x_ref, o_ref, y_ref, go_ref, gy_ref, wo_ref, g2_ref, wg_ref, wu_ref, wd_ref, x3_ref):
    ho = _rms(o_ref[...], go_ref[...]).astype(bf16)
    hy = _rms(y_ref[...], gy_ref[...]).astype(bf16)
    x2 = x_ref[...] + _dot(ho, wo_ref[:MLA_W, :]) + _dot(hy, wo_ref[MLA_W:, :])
    x3_ref[...] = _swiglu_half(x2, g2_ref[...], wg_ref, wu_ref, wd_ref)


def _mla_prep_kernel(pm_ref, cos_ref, sin_ref, gq_ref, gkv_ref, wq_ref, wk_ref, wvt_ref, qn_ref, kn_ref,
                     q_ref, k_ref, vt_ref):
    pm = pm_ref[...]
    q_lat = pm[:, :Q_LORA]
    kv_lat = pm[:, Q_LORA:Q_LORA + KV_LORA]
    kpe = pm[:, Q_LORA + KV_LORA:Q_LORA + KV_LORA + LANES]
    kpe_sw = pm[:, Q_LORA + KV_LORA + LANES:]
    cos_t = cos_ref[...]
    sin_t = sin_ref[...]
    q2 = _dot(_rms(q_lat, gq_ref[...]).astype(bf16), wq_ref[...])
    hkv = _rms(kv_lat, gkv_ref[...]).astype(bf16)
    kv = _dot(hkv, wk_ref[...])
    vt_ref[...] = _dot_nt(wvt_ref[...], hkv).astype(bf16)
    qc = qn_ref[0:1, :] * cos_t
    qs = qn_ref[1:2, :] * sin_t
    kc = kn_ref[0:1, :] * cos_t
    ks = kn_ref[1:2, :] * sin_t
    scale = MLA_QK ** -0.5 * LOG2_E
    for h in range(MLA_HEADS):
        sl = slice(h * HEAD_PAD, (h + 1) * HEAD_PAD)
        qh = q2[:, sl]
        qsw = q2[:, QK_PAD_W + h * HEAD_PAD:QK_PAD_W + (h + 1) * HEAD_PAD]
        rs = lax.rsqrt(jnp.sum(qh * qh, axis=-1, keepdims=True) * (1.0 / MLA_QK) + EPS)
        q_ref[:, sl] = ((qh * qc + qsw * qs) * (rs * scale)).astype(bf16)
        kh = kv[:, sl] + kpe
        rk = lax.rsqrt(jnp.sum(kh * kh, axis=-1, keepdims=True) * (1.0 / MLA_QK) + EPS)
        k_ref[:, sl] = ((kh * kc + kpe_sw * ks) * rk).astype(bf16)


def _attn_kernel(q_ref, k_ref, vt_ref, o_ref, s0_scr, s1_scr):
    n_tiles = q_ref.shape[0] // TQ
    chunk_gap = (lax.broadcasted_iota(jnp.int32, (HALF_TQ, HALF_TQ), 0) // CHUNK
                 - lax.broadcasted_iota(jnp.int32, (HALF_TQ, HALF_TQ), 1) // CHUNK)
    diag_visible = chunk_gap <= 0
    heads = [slice(h * HEAD_PAD, (h + 1) * HEAD_PAD) for h in range(2)]
    ones_rows = jnp.ones((ONES_ROWS, TQ), bf16)
    bufs = (s0_scr, s1_scr)
    tile = lambda n: slice(n * TQ, (n + 1) * TQ)
    lo_half = lambda n: slice(n * TQ, n * TQ + HALF_TQ)
    hi_half = lambda n: slice(n * TQ + HALF_TQ, (n + 1) * TQ)

    def scores(step, s_scr):
        i, j = step
        col_max = []
        for h in range(2):
            if j == i:
                top = _dot_nt(k_ref[lo_half(j), heads[h]], q_ref[tile(i), heads[h]])
                top = jnp.concatenate([jnp.where(diag_visible, top[:, :HALF_TQ], -jnp.inf), top[:, HALF_TQ:]], axis=1)
                bot = _dot_nt(k_ref[hi_half(j), heads[h]], q_ref[hi_half(i), heads[h]])
                bot = jnp.where(diag_visible, bot, -jnp.inf)
                s_scr[h, :HALF_TQ, :] = top
                s_scr[h, HALF_TQ:, HALF_TQ:] = bot
                top_max = jnp.max(top, axis=0, keepdims=True)
                col_max.append(jnp.concatenate(
                    [top_max[:, :HALF_TQ], jnp.maximum(top_max[:, HALF_TQ:], jnp.max(bot, axis=0, keepdims=True))],
                    axis=1))
            else:
                st = _dot_nt(k_ref[tile(j), heads[h]], q_ref[tile(i), heads[h]])
                s_scr[h] = st
                col_max.append(jnp.max(st, axis=0, keepdims=True))
        return col_max

    def vt_aug(h, cols):
        return jnp.concatenate([vt_ref[h * MLA_V:(h + 1) * MLA_V, cols], ones_rows[:, :cols.stop - cols.start]], axis=0)

    def consume(step, s_scr, col_max, stats):
        i, j = step
        new = []
        for h in range(2):
            m, acc = stats[h]
            m_new = jnp.maximum(m, col_max[h])
            alpha = jnp.exp2(m - m_new)
            if j == i:
                p_top = jnp.exp2(s_scr[h, :HALF_TQ, :] - m_new).astype(bf16)
                p_bot = jnp.exp2(s_scr[h, HALF_TQ:, HALF_TQ:] - m_new[:, HALF_TQ:]).astype(bf16)
                upd = _dot(vt_aug(h, lo_half(j)), p_top)
                upd_hi = _dot(vt_aug(h, hi_half(j)), p_bot)
                upd = jnp.concatenate([upd[:, :HALF_TQ], upd[:, HALF_TQ:] + upd_hi], axis=1)
            else:
                upd = _dot(vt_aug(h, tile(j)), jnp.exp2(s_scr[h] - m_new).astype(bf16))
            new.append((m_new, alpha * acc + upd))
        return tuple(new)

    steps = [(i, j) for i in range(n_tiles) for j in range(i + 1)]
    next_max = scores(steps[0], bufs[0])
    stats = None
    for n, step in enumerate(steps):
        i, j = step
        cur_max = next_max
        if n + 1 < len(steps):
            next_max = scores(steps[n + 1], bufs[(n + 1) % 2])
        if j == 0:
            stats = tuple((jnp.full((1, TQ), -jnp.inf, f32), jnp.zeros((MLA_V + ONES_ROWS, TQ), f32))
                          for _ in range(2))
        stats = consume(step, bufs[n % 2], cur_max, stats)
        if j == i:
            (_, acc0), (_, acc1) = stats
            o_ref[tile(i), :] = jnp.concatenate([acc0[:MLA_V] / acc0[MLA_V:MLA_V + 1],
                                                 acc1[:MLA_V] / acc1[MLA_V:MLA_V + 1]], axis=0).T


def _rwkv_kernel(p_ref, mu_ref, w0_ref, a0_ref, wwa_ref, g2_ref, kk_ref, ka_ref, rk_ref, gng_ref, gnb_ref,
                 e64_ref, tri_ref, y_ref,
                 shift_scr, state_scr, at_scr, rt_scr, bt_scr, kt_scr, bp_scr, kp_scr, v_scr, pc_scr):
    tm = TM_RWKV
    i = pl.program_id(1)

    @pl.when(i == 0)
    def _():
        shift_scr[0:8, :] = jnp.zeros((8, P_RWKV_W), f32)
        state_scr[...] = jnp.zeros_like(state_scr)

    cur = p_ref[...]
    shift_scr[8:8 + tm, :] = cur
    prev = shift_scr[7:7 + tm, :]
    xs = cur + mu_ref[...] * (prev - cur)
    shift_scr[0:8, :] = shift_scr[tm:tm + 8, :]

    r = xs[:, 0:RWKV_W]
    k = xs[:, RWKV_W:2 * RWKV_W]
    v = xs[:, 2 * RWKV_W:3 * RWKV_W]
    wa = xs[:, 3 * RWKV_W:3 * RWKV_W + LANES]
    gl = xs[:, 3 * RWKV_W + LANES:]

    lane = lax.broadcasted_iota(jnp.int32, (tm, LANES), 1)
    wa = jnp.where(lane < DECAY_LORA, jnp.tanh(wa), wa)
    wa_pre = _dot(wa.astype(bf16), wwa_ref[...])
    z = -(w0_ref[...] + wa_pre[:, :RWKV_W])
    softplus = jnp.maximum(z, 0.0) + jnp.log(1.0 + jnp.exp(-jnp.abs(z)))
    logd = -jnp.exp(-softplus - 0.5)
    a = jax.nn.sigmoid(a0_ref[...] + wa_pre[:, RWKV_W:])
    g = _dot(jax.nn.sigmoid(gl).astype(bf16), g2_ref[...])

    e64 = e64_ref[...]
    kk = k * kk_ref[...]
    kk = kk / jnp.maximum(jnp.sqrt(_dot((kk * kk).astype(bf16), e64)), 1e-12)
    kmod = k * (1.0 + (a - 1.0) * ka_ref[...])
    bonus = _dot((r * kmod * rk_ref[...]).astype(bf16), e64) * v

    tri_rows = tri_ref.shape[0]
    cl = jnp.concatenate([_dot_split(tri_ref[...], logd[s * tri_rows:(s + 1) * tri_rows])
                          for s in range(tm // tri_rows)], axis=0)
    cl3 = cl.reshape(tm // CHUNK, CHUNK, RWKV_W)
    cl_end = jnp.broadcast_to(cl3[:, CHUNK - 1:CHUNK, :], cl3.shape).reshape(tm, RWKV_W)
    at_scr[...] = (-kk * jnp.exp(cl - logd)).astype(bf16)
    rt_scr[...] = (r * jnp.exp(cl)).astype(bf16)
    pinv = jnp.exp(-cl)
    bt_scr[...] = (kk * a * pinv).astype(bf16)
    kt_scr[...] = (kmod * pinv).astype(bf16)
    chunk_decay = jnp.exp(cl_end)
    q_end = chunk_decay * pinv
    bp_scr[...] = (kk * a * q_end).astype(bf16)
    kp_scr[...] = (kmod * q_end).astype(bf16)
    v_scr[...] = v.astype(bf16)
    pc_scr[...] = chunk_decay

    c2 = 2 * CHUNK
    lane_c = lax.broadcasted_iota(jnp.int32, (CHUNK, c2), 1)
    row_c = lax.broadcasted_iota(jnp.int32, (CHUNK, c2), 0)
    left = lane_c < CHUNK
    s_idx = jnp.where(left, lane_c, lane_c - CHUNK)
    strict = s_idx < row_c
    incl = s_idx <= row_c
    eye_sbs = (s_idx == row_c).astype(f32)
    level_masks = [((row_c >> (lvl + 1)) == (s_idx >> (lvl + 1))) & (((row_c >> lvl) & 1) == 1)
                   & (((s_idx >> lvl) & 1) == 0) for lvl in range(6)]
    bd_mask = ((lax.broadcasted_iota(jnp.int32, (c2, c2), 0) // CHUNK)
               == (lax.broadcasted_iota(jnp.int32, (c2, c2), 1) // CHUNK))

    def stack2(x):
        zero = jnp.zeros_like(x)
        return jnp.concatenate([jnp.where(left, x, zero), jnp.where(left, zero, x)], axis=0)

    n_chunk = tm // CHUNK
    units = [(c, p) for c in range(n_chunk) for p in range(N_PAIR)]

    def region(c, p):
        return slice(c * CHUNK, (c + 1) * CHUNK), slice(p * LANES, (p + 1) * LANES)

    ar, vb, vst, a_ab, a_ak, a_r, t_inv, w1 = {}, {}, {}, {}, {}, {}, {}, {}
    for u_ in units:
        rows, cols = region(*u_)
        ar[u_] = jnp.concatenate([at_scr[rows, cols], rt_scr[rows, cols]], axis=0)
        yk = jnp.concatenate([stack2(bt_scr[rows, cols]), stack2(kt_scr[rows, cols])], axis=0)
        gm = _dot_nt(ar[u_], yk)
        a_ab[u_] = gm[:CHUNK, :c2]
        a_ak[u_] = jnp.where(strict, gm[:CHUNK, c2:], 0.0).astype(bf16)
        a_r[u_] = jnp.concatenate([jnp.where(incl, gm[CHUNK:, :c2], 0.0),
                                   jnp.where(incl, gm[CHUNK:, c2:], 0.0)], axis=1).astype(bf16)
        t_inv[u_] = eye_sbs + jnp.where(level_masks[0], a_ab[u_], 0.0)
    for lvl in range(1, 6):
        nt = {}
        for u_ in units:
            n_k = jnp.where(level_masks[lvl], a_ab[u_], 0.0).astype(bf16)
            nt[u_] = _dot(n_k, stack2(t_inv[u_].astype(bf16)))
        for u_ in units:
            t_inv[u_] = t_inv[u_] + _dot(t_inv[u_].astype(bf16), stack2(nt[u_].astype(bf16)))
    for u_ in units:
        rows, cols = region(*u_)
        vb[u_] = v_scr[rows, cols]
        vst[u_] = stack2(vb[u_])
        w1[u_] = _dot(a_ak[u_], vst[u_])
    a2, u1 = {}, {}
    for u_ in units:
        au = _dot(t_inv[u_].astype(bf16),
                  jnp.concatenate([stack2(ar[u_][:CHUNK]), stack2(w1[u_].astype(bf16))], axis=1))
        a2[u_] = au[:, :c2].astype(bf16)
        u1[u_] = au[:, c2:].astype(bf16)
    m_c, n_c, rt2, y1 = {}, {}, {}, {}
    for u_ in units:
        rows, cols = region(*u_)
        lhs = jnp.concatenate([jnp.concatenate([a2[u_], u1[u_]], axis=1),
                               jnp.concatenate([jnp.zeros_like(vb[u_]), vb[u_]], axis=1)], axis=0)
        mn = _dot_tn(lhs, jnp.concatenate([bp_scr[rows, cols], kp_scr[rows, cols]], axis=0))
        m_c[u_] = jnp.where(bd_mask, mn[:c2], 0.0).astype(bf16)
        n_c[u_] = jnp.where(bd_mask, mn[c2:], 0.0)
    for u_ in units:
        rt2[u_] = (ar[u_][CHUNK:].astype(f32) + _dot(a_r[u_][:, :c2], stack2(a2[u_]))).astype(bf16)
    for u_ in units:
        y1[u_] = _dot(a_r[u_], jnp.concatenate([stack2(u1[u_]), vst[u_]], axis=0))

    ys = [[None] * N_PAIR for _ in range(n_chunk)]
    ht = [state_scr[p] for p in range(N_PAIR)]
    for c in range(n_chunk):
        hb = [ht[p].astype(bf16) for p in range(N_PAIR)]
        for p in range(N_PAIR):
            cols = region(c, p)[1]
            ht[p] = ht[p] * pc_scr[c * CHUNK:c * CHUNK + 1, cols] + _dot(hb[p], m_c[c, p]) + n_c[c, p]
        for p in range(N_PAIR):
            ys[c][p] = _dot_nt(rt2[c, p], hb[p]) + y1[c, p]
    for p in range(N_PAIR):
        state_scr[p] = ht[p]

    y = jnp.concatenate([jnp.concatenate(row_y, axis=1) for row_y in ys], axis=0)
    inv_n = 1.0 / RWKV_HEAD
    mean = _dot(y.astype(bf16), e64) * inv_n
    yc = y - mean
    var = _dot((yc * yc).astype(bf16), e64) * inv_n
    yn = yc * lax.rsqrt(var + GN_EPS) * gng_ref[...] + gnb_ref[...]
    y_ref[...] = (yn + bonus) * g


def _const_spec(shape):
    return pl.BlockSpec(shape, lambda *_: (0,) * len(shape), pipeline_mode=pl.Buffered(1))


def _layer_spec(stacked, layer):
    rest = stacked.shape[1:]
    return pl.BlockSpec((None,) + rest, lambda *_: (layer,) + (0,) * len(rest), pipeline_mode=pl.Buffered(1))


def _params(sem):
    return pltpu.CompilerParams(dimension_semantics=sem, vmem_limit_bytes=VMEM_LIMIT)


def _ffn_in(x, g1, wg, wu, wd, gm, win, layer):
    t = x.shape[0]
    tm = TM_FFN
    row = lambda w: pl.BlockSpec((tm, w), lambda i: (i, 0))
    return pl.pallas_call(
        _ffn_in_kernel,
        grid=(t // tm,),
        in_specs=[row(D_MODEL), _const_spec(g1.shape), _layer_spec(wg, layer), _layer_spec(wu, layer),
                  _layer_spec(wd, layer), _const_spec(gm.shape), _layer_spec(win, layer)],
        out_specs=[row(D_MODEL), row(P_MLA_W), row(P_RWKV_W)],
        out_shape=[jax.ShapeDtypeStruct((t, D_MODEL), f32), jax.ShapeDtypeStruct((t, P_MLA_W), f32),
                   jax.ShapeDtypeStruct((t, P_RWKV_W), f32)],
        compiler_params=_params(("parallel",)),
        name="ffn_in",
    )(x, g1, wg, wu, wd, gm, win)


def _out_ffn(x1, o, y, go, gy, wo, g2, wg, wu, wd, layer):
    t = x1.shape[0]
    tm = TM_OUT_FFN
    row = lambda w: pl.BlockSpec((tm, w), lambda i: (i, 0))
    return pl.pallas_call(
        _out_ffn_kernel,
        grid=(t // tm,),
        in_specs=[row(D_MODEL), row(MLA_W), row(RWKV_W), _const_spec(go.shape), _const_spec(gy.shape),
                  _layer_spec(wo, layer), _const_spec(g2.shape), _layer_spec(wg, layer), _layer_spec(wu, layer),
                  _layer_spec(wd, layer)],
        out_specs=row(D_MODEL),
        out_shape=jax.ShapeDtypeStruct((t, D_MODEL), f32),
        compiler_params=_params(("parallel",)),
        name="out_ffn",
    )(x1, o, y, go, gy, wo, g2, wg, wu, wd)


def _mla_prep(pm, cos_t, sin_t, gq, gkv, wq, wk, wvt, qn, kn, seq):
    t = pm.shape[0]
    tm = TM_PREP
    n_pos = seq // tm
    row = lambda w: pl.BlockSpec((tm, w), lambda i: (i, 0))
    pos = pl.BlockSpec((tm, LANES), lambda i: (i % n_pos, 0))
    return pl.pallas_call(
        _mla_prep_kernel,
        grid=(t // tm,),
        in_specs=[row(P_MLA_W), pos, pos, _const_spec(gq.shape), _const_spec(gkv.shape), _const_spec(wq.shape),
                  _const_spec(wk.shape), _const_spec(wvt.shape), _const_spec(qn.shape), _const_spec(kn.shape)],
        out_specs=[row(QK_PAD_W), row(QK_PAD_W), pl.BlockSpec((MLA_W, tm), lambda i: (0, i))],
        out_shape=[jax.ShapeDtypeStruct((t, QK_PAD_W), bf16), jax.ShapeDtypeStruct((t, QK_PAD_W), bf16),
                   jax.ShapeDtypeStruct((MLA_W, t), bf16)],
        compiler_params=_params(("parallel",)),
        name="mla_prep",
    )(pm, cos_t, sin_t, gq, gkv, wq, wk, wvt, qn, kn)


def _attention(q, k, vt, batch, seq):
    return pl.pallas_call(
        _attn_kernel,
        grid=(batch, MLA_HEADS // 2),
        in_specs=[pl.BlockSpec((seq, 2 * HEAD_PAD), lambda b, hp: (b, hp)),
                  pl.BlockSpec((seq, 2 * HEAD_PAD), lambda b, hp: (b, hp)),
                  pl.BlockSpec((2 * MLA_V, seq), lambda b, hp: (hp, b))],
        out_specs=pl.BlockSpec((seq, 2 * MLA_V), lambda b, hp: (b, hp)),
        out_shape=jax.ShapeDtypeStruct((batch * seq, MLA_W), f32),
        scratch_shapes=[pltpu.VMEM((2, TQ, TQ), f32)] * 2,
        compiler_params=_params(("parallel", "parallel")),
        name="attention",
    )(q, k, vt)


def _rwkv(pr, mu, w0, a0, wwa, g2, kk, ka, rk, gng, gnb, e64, tri, batch, seq):
    tm = TM_RWKV
    ns = seq // tm
    vecs = [mu, w0, a0, wwa, g2, kk, ka, rk, gng, gnb, e64, tri]
    return pl.pallas_call(
        _rwkv_kernel,
        grid=(batch, ns),
        in_specs=[pl.BlockSpec((tm, P_RWKV_W), lambda b, i: (b * ns + i, 0))] + [_const_spec(a.shape) for a in vecs],
        out_specs=pl.BlockSpec((tm, RWKV_W), lambda b, i: (b * ns + i, 0)),
        out_shape=jax.ShapeDtypeStruct((batch * seq, RWKV_W), f32),
        scratch_shapes=[pltpu.VMEM((tm + 8, P_RWKV_W), f32),
                        pltpu.VMEM((N_PAIR, LANES, LANES), f32)]
                       + [pltpu.VMEM((tm, RWKV_W), bf16)] * 7
                       + [pltpu.VMEM((tm, RWKV_W), f32)],
        compiler_params=_params(("arbitrary", "arbitrary")),
        name="rwkv",
    )(pr, *vecs)


def _win_ext(w_in):
    z = lambda n: jnp.zeros(w_in.shape[:-1] + (n,), w_in.dtype)
    pe = w_in[..., Q_LORA + KV_LORA:MLA_IN]
    half = MLA_ROPE // 2
    rw = w_in[..., MLA_IN:]
    o = 0
    r_w = rw[..., o:o + RWKV_W]; o += RWKV_W
    wl_w = rw[..., o:o + DECAY_LORA]; o += DECAY_LORA
    k_w = rw[..., o:o + RWKV_W]; o += RWKV_W
    v_w = rw[..., o:o + RWKV_W]; o += RWKV_W
    al_w = rw[..., o:o + A_LORA]; o += A_LORA
    gl_w = rw[..., o:o + GATE_LORA]
    return jnp.concatenate([
        w_in[..., :Q_LORA + KV_LORA],
        z(MLA_NOPE), pe, z(LANES - MLA_QK),
        z(MLA_NOPE), pe[..., half:], pe[..., :half], z(LANES - MLA_QK),
        r_w, k_w, v_w, wl_w, al_w, gl_w, z(GATE_PAD - GATE_LORA)], axis=-1)


def _mu_ext(mu):
    o = 0
    r_m = mu[o:o + RWKV_W]; o += RWKV_W
    wl_m = mu[o:o + DECAY_LORA]; o += DECAY_LORA
    k_m = mu[o:o + RWKV_W]; o += RWKV_W
    v_m = mu[o:o + RWKV_W]; o += RWKV_W
    al_m = mu[o:o + A_LORA]; o += A_LORA
    gl_m = mu[o:o + GATE_LORA]
    return jnp.concatenate([r_m, k_m, v_m, wl_m, al_m, gl_m, jnp.zeros((GATE_PAD - GATE_LORA,), mu.dtype)])[None, :]


def _head_pad_cols(w, width):
    kdim = w.shape[0]
    w = w.reshape(kdim, MLA_HEADS, width)
    return jnp.pad(w, ((0, 0), (0, 0), (0, HEAD_PAD - width))).reshape(kdim, QK_PAD_W)


def _swap_rope_cols(w):
    kdim = w.shape[0]
    half = MLA_ROPE // 2
    w = w.reshape(kdim, MLA_HEADS, MLA_QK)
    sw = jnp.concatenate([jnp.zeros((kdim, MLA_HEADS, MLA_NOPE), w.dtype), w[..., MLA_NOPE + half:],
                          w[..., MLA_NOPE:MLA_NOPE + half],
                          jnp.zeros((kdim, MLA_HEADS, HEAD_PAD - MLA_QK), w.dtype)], axis=-1)
    return sw.reshape(kdim, QK_PAD_W)


def _norm_pair(g):
    half = MLA_ROPE // 2
    zpad = jnp.zeros((HEAD_PAD - MLA_QK,), g.dtype)
    plain = jnp.concatenate([g, zpad])
    sw = jnp.concatenate([jnp.zeros((MLA_NOPE,), g.dtype), g[MLA_NOPE + half:], g[MLA_NOPE:MLA_NOPE + half], zpad])
    return jnp.stack([plain, sw])


def _rope_tiles(seq):
    pos = jnp.arange(seq, dtype=f32)
    inv_freq = ROPE_BASE ** (-jnp.arange(0, MLA_ROPE, 2, dtype=f32) / MLA_ROPE)
    ang = pos[:, None] * inv_freq[None, :]
    cos, sin = jnp.cos(ang), jnp.sin(ang)
    zpad = jnp.zeros((seq, HEAD_PAD - MLA_QK), f32)
    cos_t = jnp.concatenate([jnp.ones((seq, MLA_NOPE), f32), cos, cos, zpad], axis=1)
    sin_t = jnp.concatenate([jnp.zeros((seq, MLA_NOPE), f32), -sin, sin, zpad], axis=1)
    return cos_t, sin_t


def _block_diag_const(n, blk, lower):
    r = jnp.arange(n)[:, None]
    c = jnp.arange(n)[None, :]
    m = (r // blk) == (c // blk)
    if lower:
        m = m & (c <= r)
    return m.astype(bf16)


@jax.jit
def kernel(x, ffn1_norm, ffn1_w_gate, ffn1_w_up, ffn1_w_down, mix_norm, w_in, q_lat_norm, w_q_up, kv_lat_norm, w_kv_up, q_norm, k_norm, shift_mu, w0, w2, a0, a2, g2, k_k, k_a, r_k, gn_gain, gn_bias, out_norm_mla, out_norm_rwkv, w_out, ffn2_norm, ffn2_w_gate, ffn2_w_up, ffn2_w_down):
    batch, seq, d = x.shape
    xt = x.reshape(batch * seq, d)
    cos_t, sin_t = _rope_tiles(seq)
    e64 = _block_diag_const(RWKV_W, RWKV_HEAD, lower=False)
    tri = _block_diag_const(TRI_ROWS, CHUNK, lower=True)
    row = lambda a: a[None, :]
    wg1, wu1, wd1 = ffn1_w_gate.astype(bf16), ffn1_w_up.astype(bf16), ffn1_w_down.astype(bf16)
    wg2, wu2, wd2 = ffn2_w_gate.astype(bf16), ffn2_w_up.astype(bf16), ffn2_w_down.astype(bf16)
    win = _win_ext(w_in.astype(bf16))
    wo = w_out.astype(bf16)
    for l in range(DEPTH):
        xt, pm, pr = _ffn_in(xt, row(ffn1_norm[l]), wg1, wu1, wd1, row(mix_norm[l]), win, l)
        wq = jnp.concatenate([_head_pad_cols(w_q_up[l], MLA_QK), _swap_rope_cols(w_q_up[l])], axis=1).astype(bf16)
        wkv3 = w_kv_up[l].reshape(KV_LORA, MLA_HEADS, MLA_NOPE + MLA_V)
        wk = _head_pad_cols(wkv3[..., :MLA_NOPE].reshape(KV_LORA, -1), MLA_NOPE).astype(bf16)
        wvt = wkv3[..., MLA_NOPE:].reshape(KV_LORA, MLA_W).T.astype(bf16)
        q, k, vt = _mla_prep(pm, cos_t, sin_t, row(q_lat_norm[l]), row(kv_lat_norm[l]), wq, wk, wvt,
                             _norm_pair(q_norm[l]), _norm_pair(k_norm[l]), seq)
        o = _attention(q, k, vt, batch, seq)
        zl = jnp.zeros((DECAY_LORA, RWKV_W), f32)
        wwa = jnp.concatenate([jnp.concatenate([w2[l], zl], axis=1), jnp.concatenate([zl, a2[l]], axis=1)], axis=0)
        g2p = jnp.pad(g2[l], ((0, GATE_PAD - GATE_LORA), (0, 0)))
        y = _rwkv(pr, _mu_ext(shift_mu[l]), row(w0[l]), row(a0[l]), wwa.astype(bf16), g2p.astype(bf16),
                  row(k_k[l]), row(k_a[l]), row(r_k[l].reshape(-1)), row(gn_gain[l]), row(gn_bias[l]),
                  e64, tri, batch, seq)
        xt = _out_ffn(xt, o, y, row(out_norm_mla[l]), row(out_norm_rwkv[l]), wo, row(ffn2_norm[l]), wg2, wu2, wd2, l)
    return xt.reshape(batch, seq, d)
```

```python
import jax
import jax.numpy as jnp
from jax import lax
from jax.experimental import pallas as pl
from jax.experimental.pallas import tpu as pltpu

f32 = jnp.float32
bf16 = jnp.bfloat16

D_MODEL = 1024
DEPTH = 4
CHUNK = 64
D_FF = 2816
EPS = 1e-6
MLA_HEADS = 8
MLA_NOPE = 64
MLA_ROPE = 32
MLA_QK = MLA_NOPE + MLA_ROPE
MLA_V = 64
Q_LORA = 256
KV_LORA = 128
ROPE_BASE = 10000.0
RWKV_HEADS = 8
RWKV_HEAD = 64
RWKV_W = RWKV_HEADS * RWKV_HEAD
DECAY_LORA = 64
A_LORA = 64
GATE_LORA = 160
GN_EPS = 64e-5
MLA_W = MLA_HEADS * MLA_V
MLA_IN = Q_LORA + KV_LORA + MLA_ROPE

LANES = 128
HEAD_PAD = LANES
QK_PAD_W = MLA_HEADS * HEAD_PAD
P_MLA_W = Q_LORA + KV_LORA + 2 * LANES
GATE_PAD = 2 * LANES
P_RWKV_W = 3 * RWKV_W + LANES + GATE_PAD
VMEM_LIMIT = 56 * 1024 * 1024

TM_FFN = 512
FF_SLAB = 1536
TM_OUT_FFN = 512
TM_PREP = 1024
TQ = 512
HALF_TQ = TQ // 2
ONES_ROWS = 16
LOG2_E = 1.4426950408889634
TM_RWKV = 1024
TRI_ROWS = 256
N_PAIR = RWKV_HEADS // 2


def _rms(x, g):
    return x * lax.rsqrt(jnp.mean(x * x, axis=-1, keepdims=True) + EPS) * g


def _dot(a, b):
    return jnp.dot(a, b, preferred_element_type=f32)


def _dot_nt(a, b):
    return lax.dot_general(a, b, (((1,), (1,)), ((), ())), preferred_element_type=f32)


def _dot_tn(a, b):
    return lax.dot_general(a, b, (((0,), (0,)), ((), ())), preferred_element_type=f32)


def _dot_split(c, x):
    hi = x.astype(bf16)
    lo = (x - hi.astype(f32)).astype(bf16)
    return _dot(c, hi) + _dot(c, lo)


def _swiglu_half(x, g, wg_ref, wu_ref, wd_ref):
    h = _rms(x, g).astype(bf16)
    gate = _dot(h, wg_ref[...])
    up = _dot(h, wu_ref[...])
    act = (jax.nn.silu(gate) * up).astype(bf16)
    return x + 0.5 * _dot(act, wd_ref[...])


def _ffn_in_kernel(x_ref, g1_ref, wg_ref, wu_ref, wd_ref, gm_ref, win_ref, x1_ref, pm_ref, pr_ref):
    x = x_ref[...]
    h = _rms(x, g1_ref[...]).astype(bf16)
    down = None
    for cols in (slice(0, FF_SLAB), slice(FF_SLAB, D_FF)):
        act = (jax.nn.silu(_dot(h, wg_ref[:, cols])) * _dot(h, wu_ref[:, cols])).astype(bf16)
        part = _dot(act, wd_ref[cols, :])
        down = part if down is None else down + part
    x1 = x + 0.5 * down
    x1_ref[...] = x1
    p = _dot(_rms(x1, gm_ref[...]).astype(bf16), win_ref[...])
    pm_ref[...] = p[:, :P_MLA_W]
    pr_ref[...] = p[:, P_MLA_W:]


def _out_ffn_kernel(x_ref, o_ref, y_ref, go_ref, gy_ref, wo_ref, g2_ref, wg_ref, wu_ref, wd_ref, x3_ref):
    ho = _rms(o_ref[...], go_ref[...]).astype(bf16)
    hy = _rms(y_ref[...], gy_ref[...]).astype(bf16)
    x2 = x_ref[...] + _dot(ho, wo_ref[:MLA_W, :]) + _dot(hy, wo_ref[MLA_W:, :])
    x3_ref[...] = _swiglu_half(x2, g2_ref[...], wg_ref, wu_ref, wd_ref)


def _mla_prep_kernel(pm_ref, cos_ref, sin_ref, gq_ref, gkv_ref, wq_ref, wk_ref, wvt_ref, qn_ref, kn_ref,
                     q_ref, k_ref, vt_ref):
    pm = pm_ref[...]
    q_lat = pm[:, :Q_LORA]
    kv_lat = pm[:, Q_LORA:Q_LORA + KV_LORA]
    kpe = pm[:, Q_LORA + KV_LORA:Q_LORA + KV_LORA + LANES]
    kpe_sw = pm[:, Q_LORA + KV_LORA + LANES:]
    cos_t = cos_ref[...]
    sin_t = sin_ref[...]
    q2 = _dot(_rms(q_lat, gq_ref[...]).astype(bf16), wq_ref[...])
    hkv = _rms(kv_lat, gkv_ref[...]).astype(bf16)
    kv = _dot(hkv, wk_ref[...])
    vt_ref[...] = _dot_nt(wvt_ref[...], hkv).astype(bf16)
    qc = qn_ref[0:1, :] * cos_t
    qs = qn_ref[1:2, :] * sin_t
    kc = kn_ref[0:1, :] * cos_t
    ks = kn_ref[1:2, :] * sin_t
    scale = MLA_QK ** -0.5 * LOG2_E
    for h in range(MLA_HEADS):
        sl = slice(h * HEAD_PAD, (h + 1) * HEAD_PAD)
        qh = q2[:, sl]
        qsw = q2[:, QK_PAD_W + h * HEAD_PAD:QK_PAD_W + (h + 1) * HEAD_PAD]
        rs = lax.rsqrt(jnp.sum(qh * qh, axis=-1, keepdims=True) * (1.0 / MLA_QK) + EPS)
        q_ref[:, sl] = ((qh * qc + qsw * qs) * (rs * scale)).astype(bf16)
        kh = kv[:, sl] + kpe
        rk = lax.rsqrt(jnp.sum(kh * kh, axis=-1, keepdims=True) * (1.0 / MLA_QK) + EPS)
        k_ref[:, sl] = ((kh * kc + kpe_sw * ks) * rk).astype(bf16)


def _attn_kernel(q_ref, k_ref, vt_ref, o_ref, s0_scr, s1_scr):
    n_tiles = q_ref.shape[0] // TQ
    chunk_gap = (lax.broadcasted_iota(jnp.int32, (HALF_TQ, HALF_TQ), 0) // CHUNK
                 - lax.broadcasted_iota(jnp.int32, (HALF_TQ, HALF_TQ), 1) // CHUNK)
    diag_visible = chunk_gap <= 0
    heads = [slice(h * HEAD_PAD, (h + 1) * HEAD_PAD) for h in range(2)]
    ones_rows = jnp.ones((ONES_ROWS, TQ), bf16)
    bufs = (s0_scr, s1_scr)
    tile = lambda n: slice(n * TQ, (n + 1) * TQ)
    lo_half = lambda n: slice(n * TQ, n * TQ + HALF_TQ)
    hi_half = lambda n: slice(n * TQ + HALF_TQ, (n + 1) * TQ)

    def scores(step, s_scr):
        i, j = step
        col_max = []
        for h in range(2):
            if j == i:
                top = _dot_nt(k_ref[lo_half(j), heads[h]], q_ref[tile(i), heads[h]])
                top = jnp.concatenate([jnp.where(diag_visible, top[:, :HALF_TQ], -jnp.inf), top[:, HALF_TQ:]], axis=1)
                bot = _dot_nt(k_ref[hi_half(j), heads[h]], q_ref[hi_half(i), heads[h]])
                bot = jnp.where(diag_visible, bot, -jnp.inf)
                s_scr[h, :HALF_TQ, :] = top
                s_scr[h, HALF_TQ:, HALF_TQ:] = bot
                top_max = jnp.max(top, axis=0, keepdims=True)
                col_max.append(jnp.concatenate(
                    [top_max[:, :HALF_TQ], jnp.maximum(top_max[:, HALF_TQ:], jnp.max(bot, axis=0, keepdims=True))],
                    axis=1))
            else:
                st = _dot_nt(k_ref[tile(j), heads[h]], q_ref[tile(i), heads[h]])
                s_scr[h] = st
                col_max.append(jnp.max(st, axis=0, keepdims=True))
        return col_max

    def vt_aug(h, cols):
        return jnp.concatenate([vt_ref[h * MLA_V:(h + 1) * MLA_V, cols], ones_rows[:, :cols.stop - cols.start]], axis=0)

    def consume(step, s_scr, col_max, stats):
        i, j = step
        new = []
        for h in range(2):
            m, acc = stats[h]
            m_new = jnp.maximum(m, col_max[h])
            alpha = jnp.exp2(m - m_new)
            if j == i:
                p_top = jnp.exp2(s_scr[h, :HALF_TQ, :] - m_new).astype(bf16)
                p_bot = jnp.exp2(s_scr[h, HALF_TQ:, HALF_TQ:] - m_new[:, HALF_TQ:]).astype(bf16)
                upd = _dot(vt_aug(h, lo_half(j)), p_top)
                upd_hi = _dot(vt_aug(h, hi_half(j)), p_bot)
                upd = jnp.concatenate([upd[:, :HALF_TQ], upd[:, HALF_TQ:] + upd_hi], axis=1)
            else:
                upd = _dot(vt_aug(h, tile(j)), jnp.exp2(s_scr[h] - m_new).astype(bf16))
            new.append((m_new, alpha * acc + upd))
        return tuple(new)

    steps = [(i, j) for i in range(n_tiles) for j in range(i + 1)]
    next_max = scores(steps[0], bufs[0])
    stats = None
    for n, step in enumerate(steps):
        i, j = step
        cur_max = next_max
        if n + 1 < len(steps):
            next_max = scores(steps[n + 1], bufs[(n + 1) % 2])
        if j == 0:
            stats = tuple((jnp.full((1, TQ), -jnp.inf, f32), jnp.zeros((MLA_V + ONES_ROWS, TQ), f32))
                          for _ in range(2))
        stats = consume(step, bufs[n % 2], cur_max, stats)
        if j == i:
            (_, acc0), (_, acc1) = stats
            o_ref[tile(i), :] = jnp.concatenate([acc0[:MLA_V] / acc0[MLA_V:MLA_V + 1],
                                                 acc1[:MLA_V] / acc1[MLA_V:MLA_V + 1]], axis=0).T


def _rwkv_kernel(p_ref, mu_ref, w0_ref, a0_ref, wwa_ref, g2_ref, kk_ref, ka_ref, rk_ref, gng_ref, gnb_ref,
                 e64_ref, tri_ref, y_ref,
                 shift_scr, state_scr, at_scr, rt_scr, bt_scr, kt_scr, bp_scr, kp_scr, v_scr, pc_scr):
    tm = TM_RWKV
    i = pl.program_id(1)

    @pl.when(i == 0)
    def _():
        shift_scr[0:8, :] = jnp.zeros((8, P_RWKV_W), f32)
        state_scr[...] = jnp.zeros_like(state_scr)

    cur = p_ref[...]
    shift_scr[8:8 + tm, :] = cur
    prev = shift_scr[7:7 + tm, :]
    xs = cur + mu_ref[...] * (prev - cur)
    shift_scr[0:8, :] = shift_scr[tm:tm + 8, :]

    r = xs[:, 0:RWKV_W]
    k = xs[:, RWKV_W:2 * RWKV_W]
    v = xs[:, 2 * RWKV_W:3 * RWKV_W]
    wa = xs[:, 3 * RWKV_W:3 * RWKV_W + LANES]
    gl = xs[:, 3 * RWKV_W + LANES:]

    lane = lax.broadcasted_iota(jnp.int32, (tm, LANES), 1)
    wa = jnp.where(lane < DECAY_LORA, jnp.tanh(wa), wa)
    wa_pre = _dot(wa.astype(bf16), wwa_ref[...])
    z = -(w0_ref[...] + wa_pre[:, :RWKV_W])
    softplus = jnp.maximum(z, 0.0) + jnp.log(1.0 + jnp.exp(-jnp.abs(z)))
    logd = -jnp.exp(-softplus - 0.5)
    a = jax.nn.sigmoid(a0_ref[...] + wa_pre[:, RWKV_W:])
    g = _dot(jax.nn.sigmoid(gl).astype(bf16), g2_ref[...])

    e64 = e64_ref[...]
    kk = k * kk_ref[...]
    kk = kk / jnp.maximum(jnp.sqrt(_dot((kk * kk).astype(bf16), e64)), 1e-12)
    kmod = k * (1.0 + (a - 1.0) * ka_ref[...])
    bonus = _dot((r * kmod * rk_ref[...]).astype(bf16), e64) * v

    tri_rows = tri_ref.shape[0]
    cl = jnp.concatenate([_dot_split(tri_ref[...], logd[s * tri_rows:(s + 1) * tri_rows])
                          for s in range(tm // tri_rows)], axis=0)
    cl3 = cl.reshape(tm // CHUNK, CHUNK, RWKV_W)
    cl_end = jnp.broadcast_to(cl3[:, CHUNK - 1:CHUNK, :], cl3.shape).reshape(tm, RWKV_W)
    at_scr[...] = (-kk * jnp.exp(cl - logd)).astype(bf16)
    rt_scr[...] = (r * jnp.exp(cl)).astype(bf16)
    pinv = jnp.exp(-cl)
    bt_scr[...] = (kk * a * pinv).astype(bf16)
    kt_scr[...] = (kmod * pinv).astype(bf16)
    chunk_decay = jnp.exp(cl_end)
    q_end = chunk_decay * pinv
    bp_scr[...] = (kk * a * q_end).astype(bf16)
    kp_scr[...] = (kmod * q_end).astype(bf16)
    v_scr[...] = v.astype(bf16)
    pc_scr[...] = chunk_decay

    c2 = 2 * CHUNK
    lane_c = lax.broadcasted_iota(jnp.int32, (CHUNK, c2), 1)
    row_c = lax.broadcasted_iota(jnp.int32, (CHUNK, c2), 0)
    left = lane_c < CHUNK
    s_idx = jnp.where(left, lane_c, lane_c - CHUNK)
    strict = s_idx < row_c
    incl = s_idx <= row_c
    eye_sbs = (s_idx == row_c).astype(f32)
    level_masks = [((row_c >> (lvl + 1)) == (s_idx >> (lvl + 1))) & (((row_c >> lvl) & 1) == 1)
                   & (((s_idx >> lvl) & 1) == 0) for lvl in range(6)]
    bd_mask = ((lax.broadcasted_iota(jnp.int32, (c2, c2), 0) // CHUNK)
               == (lax.broadcasted_iota(jnp.int32, (c2, c2), 1) // CHUNK))

    def stack2(x):
        zero = jnp.zeros_like(x)
        return jnp.concatenate([jnp.where(left, x, zero), jnp.where(left, zero, x)], axis=0)

    n_chunk = tm // CHUNK
    units = [(c, p) for c in range(n_chunk) for p in range(N_PAIR)]

    def region(c, p):
        return slice(c * CHUNK, (c + 1) * CHUNK), slice(p * LANES, (p + 1) * LANES)

    ar, vb, vst, a_ab, a_ak, a_r, t_inv, w1 = {}, {}, {}, {}, {}, {}, {}, {}
    for u_ in units:
        rows, cols = region(*u_)
        ar[u_] = jnp.concatenate([at_scr[rows, cols], rt_scr[rows, cols]], axis=0)
        yk = jnp.concatenate([stack2(bt_scr[rows, cols]), stack2(kt_scr[rows, cols])], axis=0)
        gm = _dot_nt(ar[u_], yk)
        a_ab[u_] = gm[:CHUNK, :c2]
        a_ak[u_] = jnp.where(strict, gm[:CHUNK, c2:], 0.0).astype(bf16)
        a_r[u_] = jnp.concatenate([jnp.where(incl, gm[CHUNK:, :c2], 0.0),
                                   jnp.where(incl, gm[CHUNK:, c2:], 0.0)], axis=1).astype(bf16)
        t_inv[u_] = eye_sbs + jnp.where(level_masks[0], a_ab[u_], 0.0)
    for lvl in range(1, 6):
        nt = {}
        for u_ in units:
            n_k = jnp.where(level_masks[lvl], a_ab[u_], 0.0).astype(bf16)
            nt[u_] = _dot(n_k, stack2(t_inv[u_].astype(bf16)))
        for u_ in units:
            t_inv[u_] = t_inv[u_] + _dot(t_inv[u_].astype(bf16), stack2(nt[u_].astype(bf16)))
    for u_ in units:
        rows, cols = region(*u_)
        vb[u_] = v_scr[rows, cols]
        vst[u_] = stack2(vb[u_])
        w1[u_] = _dot(a_ak[u_], vst[u_])
    a2, u1 = {}, {}
    for u_ in units:
        au = _dot(t_inv[u_].astype(bf16),
                  jnp.concatenate([stack2(ar[u_][:CHUNK]), stack2(w1[u_].astype(bf16))], axis=1))
        a2[u_] = au[:, :c2].astype(bf16)
        u1[u_] = au[:, c2:].astype(bf16)
    m_c, n_c, rt2, y1 = {}, {}, {}, {}
    for u_ in units:
        rows, cols = region(*u_)
        lhs = jnp.concatenate([jnp.concatenate([a2[u_], u1[u_]], axis=1),
                               jnp.concatenate([jnp.zeros_like(vb[u_]), vb[u_]], axis=1)], axis=0)
        mn = _dot_tn(lhs, jnp.concatenate([bp_scr[rows, cols], kp_scr[rows, cols]], axis=0))
        m_c[u_] = jnp.where(bd_mask, mn[:c2], 0.0).astype(bf16)
        n_c[u_] = jnp.where(bd_mask, mn[c2:], 0.0)
    for u_ in units:
        rt2[u_] = (ar[u_][CHUNK:].astype(f32) + _dot(a_r[u_][:, :c2], stack2(a2[u_]))).astype(bf16)
    for u_ in units:
        y1[u_] = _dot(a_r[u_], jnp.concatenate([stack2(u1[u_]), vst[u_]], axis=0))

    ys = [[None] * N_PAIR for _ in range(n_chunk)]
    ht = [state_scr[p] for p in range(N_PAIR)]
    for c in range(n_chunk):
        hb = [ht[p].astype(bf16) for p in range(N_PAIR)]
        for p in range(N_PAIR):
            cols = region(c, p)[1]
            ht[p] = ht[p] * pc_scr[c * CHUNK:c * CHUNK + 1, cols] + _dot(hb[p], m_c[c, p]) + n_c[c, p]
        for p in range(N_PAIR):
            ys[c][p] = _dot_nt(rt2[c, p], hb[p]) + y1[c, p]
    for p in range(N_PAIR):
        state_scr[p] = ht[p]

    y = jnp.concatenate([jnp.concatenate(row_y, axis=1) for row_y in ys], axis=0)
    inv_n = 1.0 / RWKV_HEAD
    mean = _dot(y.astype(bf16), e64) * inv_n
    yc = y - mean
    var = _dot((yc * yc).astype(bf16), e64) * inv_n
    yn = yc * lax.rsqrt(var + GN_EPS) * gng_ref[...] + gnb_ref[...]
    y_ref[...] = (yn + bonus) * g


def _const_spec(shape):
    return pl.BlockSpec(shape, lambda *_: (0,) * len(shape), pipeline_mode=pl.Buffered(1))


def _layer_spec(stacked, layer):
    rest = stacked.shape[1:]
    return pl.BlockSpec((None,) + rest, lambda *_: (layer,) + (0,) * len(rest), pipeline_mode=pl.Buffered(1))


def _params(sem):
    return pltpu.CompilerParams(dimension_semantics=sem, vmem_limit_bytes=VMEM_LIMIT)


def _ffn_in(x, g1, wg, wu, wd, gm, win, layer):
    t = x.shape[0]
    tm = TM_FFN
    row = lambda w: pl.BlockSpec((tm, w), lambda i: (i, 0))
    return pl.pallas_call(
        _ffn_in_kernel,
        grid=(t // tm,),
        in_specs=[row(D_MODEL), _const_spec(g1.shape), _layer_spec(wg, layer), _layer_spec(wu, layer),
                  _layer_spec(wd, layer), _const_spec(gm.shape), _layer_spec(win, layer)],
        out_specs=[row(D_MODEL), row(P_MLA_W), row(P_RWKV_W)],
        out_shape=[jax.ShapeDtypeStruct((t, D_MODEL), f32), jax.ShapeDtypeStruct((t, P_MLA_W), f32),
                   jax.ShapeDtypeStruct((t, P_RWKV_W), f32)],
        compiler_params=_params(("parallel",)),
        name="ffn_in",
    )(x, g1, wg, wu, wd, gm, win)


def _out_ffn(x1, o, y, go, gy, wo, g2, wg, wu, wd, layer):
    t = x1.shape[0]
    tm = TM_OUT_FFN
    row = lambda w: pl.BlockSpec((tm, w), lambda i: (i, 0))
    return pl.pallas_call(
        _out_ffn_kernel,
        grid=(t // tm,),
        in_specs=[row(D_MODEL), row(MLA_W), row(RWKV_W), _const_spec(go.shape), _const_spec(gy.shape),
                  _layer_spec(wo, layer), _const_spec(g2.shape), _layer_spec(wg, layer), _layer_spec(wu, layer),
                  _layer_spec(wd, layer)],
        out_specs=row(D_MODEL),
        out_shape=jax.ShapeDtypeStruct((t, D_MODEL), f32),
        compiler_params=_params(("parallel",)),
        name="out_ffn",
    )(x1, o, y, go, gy, wo, g2, wg, wu, wd)


def _mla_prep(pm, cos_t, sin_t, gq, gkv, wq, wk, wvt, qn, kn, seq):
    t = pm.shape[0]
    tm = TM_PREP
    n_pos = seq // tm
    row = lambda w: pl.BlockSpec((tm, w), lambda i: (i, 0))
    pos = pl.BlockSpec((tm, LANES), lambda i: (i % n_pos, 0))
    return pl.pallas_call(
        _mla_prep_kernel,
        grid=(t // tm,),
        in_specs=[row(P_MLA_W), pos, pos, _const_spec(gq.shape), _const_spec(gkv.shape), _const_spec(wq.shape),
                  _const_spec(wk.shape), _const_spec(wvt.shape), _const_spec(qn.shape), _const_spec(kn.shape)],
        out_specs=[row(QK_PAD_W), row(QK_PAD_W), pl.BlockSpec((MLA_W, tm), lambda i: (0, i))],
        out_shape=[jax.ShapeDtypeStruct((t, QK_PAD_W), bf16), jax.ShapeDtypeStruct((t, QK_PAD_W), bf16),
                   jax.ShapeDtypeStruct((MLA_W, t), bf16)],
        compiler_params=_params(("parallel",)),
        name="mla_prep",
    )(pm, cos_t, sin_t, gq, gkv, wq, wk, wvt, qn, kn)


def _attention(q, k, vt, batch, seq):
    return pl.pallas_call(
        _attn_kernel,
        grid=(batch, MLA_HEADS // 2),
        in_specs=[pl.BlockSpec((seq, 2 * HEAD_PAD), lambda b, hp: (b, hp)),
                  pl.BlockSpec((seq, 2 * HEAD_PAD), lambda b, hp: (b, hp)),
                  pl.BlockSpec((2 * MLA_V, seq), lambda b, hp: (hp, b))],
        out_specs=pl.BlockSpec((seq, 2 * MLA_V), lambda b, hp: (b, hp)),
        out_shape=jax.ShapeDtypeStruct((batch * seq, MLA_W), f32),
        scratch_shapes=[pltpu.VMEM((2, TQ, TQ), f32)] * 2,
        compiler_params=_params(("parallel", "parallel")),
        name="attention",
    )(q, k, vt)


def _rwkv(pr, mu, w0, a0, wwa, g2, kk, ka, rk, gng, gnb, e64, tri, batch, seq):
    tm = TM_RWKV
    ns = seq // tm
    vecs = [mu, w0, a0, wwa, g2, kk, ka, rk, gng, gnb, e64, tri]
    return pl.pallas_call(
        _rwkv_kernel,
        grid=(batch, ns),
        in_specs=[pl.BlockSpec((tm, P_RWKV_W), lambda b, i: (b * ns + i, 0))] + [_const_spec(a.shape) for a in vecs],
        out_specs=pl.BlockSpec((tm, RWKV_W), lambda b, i: (b * ns + i, 0)),
        out_shape=jax.ShapeDtypeStruct((batch * seq, RWKV_W), f32),
        scratch_shapes=[pltpu.VMEM((tm + 8, P_RWKV_W), f32),
                        pltpu.VMEM((N_PAIR, LANES, LANES), f32)]
                       + [pltpu.VMEM((tm, RWKV_W), bf16)] * 7
                       + [pltpu.VMEM((tm, RWKV_W), f32)],
        compiler_params=_params(("arbitrary", "arbitrary")),
        name="rwkv",
    )(pr, *vecs)


def _win_ext(w_in):
    z = lambda n: jnp.zeros(w_in.shape[:-1] + (n,), w_in.dtype)
    pe = w_in[..., Q_LORA + KV_LORA:MLA_IN]
    half = MLA_ROPE // 2
    rw = w_in[..., MLA_IN:]
    o = 0
    r_w = rw[..., o:o + RWKV_W]; o += RWKV_W
    wl_w = rw[..., o:o + DECAY_LORA]; o += DECAY_LORA
    k_w = rw[..., o:o + RWKV_W]; o += RWKV_W
    v_w = rw[..., o:o + RWKV_W]; o += RWKV_W
    al_w = rw[..., o:o + A_LORA]; o += A_LORA
    gl_w = rw[..., o:o + GATE_LORA]
    return jnp.concatenate([
        w_in[..., :Q_LORA + KV_LORA],
        z(MLA_NOPE), pe, z(LANES - MLA_QK),
        z(MLA_NOPE), pe[..., half:], pe[..., :half], z(LANES - MLA_QK),
        r_w, k_w, v_w, wl_w, al_w, gl_w, z(GATE_PAD - GATE_LORA)], axis=-1)


def _mu_ext(mu):
    o = 0
    r_m = mu[o:o + RWKV_W]; o += RWKV_W
    wl_m = mu[o:o + DECAY_LORA]; o += DECAY_LORA
    k_m = mu[o:o + RWKV_W]; o += RWKV_W
    v_m = mu[o:o + RWKV_W]; o += RWKV_W
    al_m = mu[o:o + A_LORA]; o += A_LORA
    gl_m = mu[o:o + GATE_LORA]
    return jnp.concatenate([r_m, k_m, v_m, wl_m, al_m, gl_m, jnp.zeros((GATE_PAD - GATE_LORA,), mu.dtype)])[None, :]


def _head_pad_cols(w, width):
    kdim = w.shape[0]
    w = w.reshape(kdim, MLA_HEADS, width)
    return jnp.pad(w, ((0, 0), (0, 0), (0, HEAD_PAD - width))).reshape(kdim, QK_PAD_W)


def _swap_rope_cols(w):
    kdim = w.shape[0]
    half = MLA_ROPE // 2
    w = w.reshape(kdim, MLA_HEADS, MLA_QK)
    sw = jnp.concatenate([jnp.zeros((kdim, MLA_HEADS, MLA_NOPE), w.dtype), w[..., MLA_NOPE + half:],
                          w[..., MLA_NOPE:MLA_NOPE + half],
                          jnp.zeros((kdim, MLA_HEADS, HEAD_PAD - MLA_QK), w.dtype)], axis=-1)
    return sw.reshape(kdim, QK_PAD_W)


def _norm_pair(g):
    half = MLA_ROPE // 2
    zpad = jnp.zeros((HEAD_PAD - MLA_QK,), g.dtype)
    plain = jnp.concatenate([g, zpad])
    sw = jnp.concatenate([jnp.zeros((MLA_NOPE,), g.dtype), g[MLA_NOPE + half:], g[MLA_NOPE:MLA_NOPE + half], zpad])
    return jnp.stack([plain, sw])


def _rope_tiles(seq):
    pos = jnp.arange(seq, dtype=f32)
    inv_freq = ROPE_BASE ** (-jnp.arange(0, MLA_ROPE, 2, dtype=f32) / MLA_ROPE)
    ang = pos[:, None] * inv_freq[None, :]
    cos, sin = jnp.cos(ang), jnp.sin(ang)
    zpad = jnp.zeros((seq, HEAD_PAD - MLA_QK), f32)
    cos_t = jnp.concatenate([jnp.ones((seq, MLA_NOPE), f32), cos, cos, zpad], axis=1)
    sin_t = jnp.concatenate([jnp.zeros((seq, MLA_NOPE), f32), -sin, sin, zpad], axis=1)
    return cos_t, sin_t


def _block_diag_const(n, blk, lower):
    r = jnp.arange(n)[:, None]
    c = jnp.arange(n)[None, :]
    m = (r // blk) == (c // blk)
    if lower:
        m = m & (c <= r)
    return m.astype(bf16)


@jax.jit
def kernel(x, ffn1_norm, ffn1_w_gate, ffn1_w_up, ffn1_w_down, mix_norm, w_in, q_lat_norm, w_q_up, kv_lat_norm, w_kv_up, q_norm, k_norm, shift_mu, w0, w2, a0, a2, g2, k_k, k_a, r_k, gn_gain, gn_bias, out_norm_mla, out_norm_rwkv, w_out, ffn2_norm, ffn2_w_gate, ffn2_w_up, ffn2_w_down):
    batch, seq, d = x.shape
    xt = x.reshape(batch * seq, d)
    cos_t, sin_t = _rope_tiles(seq)
    e64 = _block_diag_const(RWKV_W, RWKV_HEAD, lower=False)
    tri = _block_diag_const(TRI_ROWS, CHUNK, lower=True)
    row = lambda a: a[None, :]
    wg1, wu1, wd1 = ffn1_w_gate.astype(bf16), ffn1_w_up.astype(bf16), ffn1_w_down.astype(bf16)
    wg2, wu2, wd2 = ffn2_w_gate.astype(bf16), ffn2_w_up.astype(bf16), ffn2_w_down.astype(bf16)
    win = _win_ext(w_in.astype(bf16))
    wo = w_out.astype(bf16)
    for l in range(DEPTH):
        xt, pm, pr = _ffn_in(xt, row(ffn1_norm[l]), wg1, wu1, wd1, row(mix_norm[l]), win, l)
        wq = jnp.concatenate([_head_pad_cols(w_q_up[l], MLA_QK), _swap_rope_cols(w_q_up[l])], axis=1).astype(bf16)
        wkv3 = w_kv_up[l].reshape(KV_LORA, MLA_HEADS, MLA_NOPE + MLA_V)
        wk = _head_pad_cols(wkv3[..., :MLA_NOPE].reshape(KV_LORA, -1), MLA_NOPE).astype(bf16)
        wvt = wkv3[..., MLA_NOPE:].reshape(KV_LORA, MLA_W).T.astype(bf16)
        q, k, vt = _mla_prep(pm, cos_t, sin_t, row(q_lat_norm[l]), row(kv_lat_norm[l]), wq, wk, wvt,
                             _norm_pair(q_norm[l]), _norm_pair(k_norm[l]), seq)
        o = _attention(q, k, vt, batch, seq)
        zl = jnp.zeros((DECAY_LORA, RWKV_W), f32)
        wwa = jnp.concatenate([jnp.concatenate([w2[l], zl], axis=1), jnp.concatenate([zl, a2[l]], axis=1)], axis=0)
        g2p = jnp.pad(g2[l], ((0, GATE_PAD - GATE_LORA), (0, 0)))
        y = _rwkv(pr, _mu_ext(shift_mu[l]), row(w0[l]), row(a0[l]), wwa.astype(bf16), g2p.astype(bf16),
                  row(k_k[l]), row(k_a[l]), row(r_k[l].reshape(-1)), row(gn_gain[l]), row(gn_bias[l]),
                  e64, tri, batch, seq)
        xt = _out_ffn(xt, o, y, row(out_norm_mla[l]), row(out_norm_rwkv[l]), wo, row(ffn2_norm[l]), wg2, wu2, wd2, l)
    return xt.reshape(batch, seq, d)
```
